```python
import math
import jax, jax.numpy as jnp
from jax import lax
import numpy as np

D_MODEL = 1024
BATCH = 8
SEQ = 2048
DEPTH = 4
DEC_BATCH = 32
DEC_SEQ = 1
PAST_LEN = 8192
PAGE_SIZE = 128

SB_HEADS = 8
SB_HEAD_DIM = 64
SB_WIDTH = SB_HEADS * SB_HEAD_DIM
DIFF_HEADS = 4
DIFF_HEAD_DIM = 64
DIFF_V_DIM = 2 * DIFF_HEAD_DIM
DIFF_QK_WIDTH = DIFF_HEADS * DIFF_HEAD_DIM
DIFF_V_WIDTH = DIFF_HEADS * DIFF_V_DIM
N_BRANCH = 2
IN_WIDTH = 3 * SB_WIDTH + 4 * DIFF_QK_WIDTH + DIFF_V_WIDTH + N_BRANCH * D_MODEL
D_FF = 2816
N_EXPERTS = 8
TOP_K = 2
D_FF_EXPERT = 2816
N_DENSE = (DEPTH + 1) // 2
N_MOE = DEPTH // 2
N_BUCKETS = 32
MAX_DISTANCE = 128
Q_BLOCK = 128
ALPHA = (2.0 * DEPTH) ** 0.25
BETA = (8.0 * DEPTH) ** -0.25
LN_EPS = 1e-5
SUBLN_EPS = 1e-5
NEG_INF = -1e30

kernel_name = "stickbreak_diffattn_gated_hybrid_step"


def layer_norm(x, g, b):
    xf = x.astype(jnp.float32)
    mu = jnp.mean(xf, axis=-1, keepdims=True)
    var = jnp.mean(jnp.square(xf - mu), axis=-1, keepdims=True)
    y = (xf - mu) * lax.rsqrt(var + LN_EPS) * g.astype(jnp.float32) + b.astype(jnp.float32)
    return y.astype(x.dtype)


def t5_bucket(dist):
    n = jnp.maximum(dist, 0)
    max_exact = N_BUCKETS // 2
    nf = jnp.maximum(n, 1).astype(jnp.float32)
    large = max_exact + (jnp.log(nf / max_exact) / math.log(MAX_DISTANCE / max_exact)
                         * (N_BUCKETS - max_exact)).astype(jnp.int32)
    large = jnp.minimum(large, N_BUCKETS - 1)
    return jnp.where(n < max_exact, n, large)


def project_in(x, w_in):
    B, T, _ = x.shape
    p = x @ w_in
    cuts = np.cumsum([SB_WIDTH] * 3 + [DIFF_QK_WIDTH] * 4 + [DIFF_V_WIDTH]).tolist()
    q_sb, k_sb, v_sb, q1, q2, k1, k2, v_d, gate = jnp.split(p, cuts, axis=-1)
    sb = lambda a: a.reshape(B, T, SB_HEADS, SB_HEAD_DIM)
    dq = lambda a: a.reshape(B, T, DIFF_HEADS, DIFF_HEAD_DIM)
    return (sb(q_sb), sb(k_sb), sb(v_sb), dq(q1), dq(q2), dq(k1), dq(k2),
            v_d.reshape(B, T, DIFF_HEADS, DIFF_V_DIM), gate.reshape(B, T, N_BRANCH, D_MODEL))


def sb_attend(q, k, v, q_pos, k_pos):
    z = jnp.einsum('bqhd,bkhd->bhqk', q, k).astype(jnp.float32) / math.sqrt(q.shape[-1])
    causal = k_pos[None, :] < q_pos[:, None]
    log_beta = jax.nn.log_sigmoid(z)
    log_one_minus = jnp.where(causal, jax.nn.log_sigmoid(-z), 0.0)
    rest = lax.cumsum(log_one_minus, axis=3, reverse=True) - log_one_minus
    w = jnp.where(causal, jnp.exp(log_beta + rest), 0.0)
    return jnp.einsum('bhqk,bkhd->bqhd', w.astype(v.dtype), v)


def diff_attend(q1, q2, k1, k2, v, q_pos, k_pos, rel_bias, lam, lam_init, subln_g):
    scale = 1.0 / math.sqrt(q1.shape[-1])
    mask = k_pos[None, :] <= q_pos[:, None]
    bias = rel_bias[t5_bucket(q_pos[:, None] - k_pos[None, :])]
    bias = jnp.transpose(bias, (2, 0, 1)).astype(jnp.float32)

    def probs(q, k):
        s = jnp.einsum('bqhd,bkhd->bhqk', q, k).astype(jnp.float32) * scale + bias
        return jax.nn.softmax(jnp.where(mask, s, NEG_INF), axis=-1)

    a = probs(q1, k1) - lam * probs(q2, k2)
    o = jnp.einsum('bhqk,bkhe->bqhe', a.astype(v.dtype), v).astype(jnp.float32)
    o = o * lax.rsqrt(jnp.mean(jnp.square(o), axis=-1, keepdims=True) + SUBLN_EPS)
    o = o * subln_g.astype(jnp.float32) * (1.0 - lam_init)
    return o.astype(v.dtype)


def sweep_query_blocks(attend, queries, q_pos):
    B, T = queries[0].shape[:2]
    nb = T // Q_BLOCK
    to_blocks = lambda a: jnp.moveaxis(a.reshape((B, nb, Q_BLOCK) + a.shape[2:]), 1, 0)
    out = lax.map(lambda a: attend(*a[0], a[1]),
                  (tuple(to_blocks(q) for q in queries), q_pos.reshape(nb, Q_BLOCK)))
    out = jnp.moveaxis(out, 0, 1)
    return out.reshape((B, T) + out.shape[3:])


def diff_lambda_value(lam_params, l):
    lp = lam_params.astype(jnp.float32)
    lam_init = 0.8 - 0.6 * math.exp(-0.3 * l)
    lam = jnp.exp(jnp.sum(lp[0] * lp[1])) - jnp.exp(jnp.sum(lp[2] * lp[3])) + lam_init
    return lam, lam_init


def merge_branches(o_sb, o_d, gate, w_pa, w_pb, w_o):
    B, T = o_sb.shape[:2]
    pa = o_sb.reshape(B, T, SB_WIDTH) @ w_pa
    pb = o_d.reshape(B, T, DIFF_V_WIDTH) @ w_pb
    g = jax.nn.sigmoid(gate)
    return (g[:, :, 0] * pa + g[:, :, 1] * pb) @ w_o


def swiglu(x, wg, wu, wd):
    return (jax.nn.silu(x @ wg) * (x @ wu)) @ wd


def moe_swiglu(x, router_w, wg, wu, wd):
    logits = (x @ router_w).astype(jnp.float32)
    top_val, top_idx = lax.top_k(logits, TOP_K)
    top_w = jax.nn.softmax(top_val, axis=-1)
    gates = jnp.sum(jax.nn.one_hot(top_idx, N_EXPERTS, dtype=jnp.float32) * top_w[..., None], axis=-2)
    y = jnp.zeros_like(x)
    for e in range(N_EXPERTS):
        y = y + gates[..., e:e + 1].astype(x.dtype) * swiglu(x, wg[e], wu[e], wd[e])
    return y


def gather_pages(cache_l, page_table):
    g = cache_l[page_table]
    return g.reshape((g.shape[0], g.shape[1] * g.shape[2]) + g.shape[3:])


def setup_inputs(seed: int = 0) -> dict:
    key = jax.random.key(seed)
    ks = jax.random.split(key, 32)
    n_pages = PAST_LEN // PAGE_SIZE
    n_used = DEC_BATCH * n_pages
    n_pool = (n_used * 5) // 4
    f32 = jnp.float32
    nrm = lambda k, shape, s: jax.random.normal(k, shape, f32) * s
    d_in = D_MODEL ** -0.5
    col_scale = jnp.concatenate([
        jnp.full((2 * SB_WIDTH,), d_in, f32), jnp.full((SB_WIDTH,), d_in * BETA, f32),
        jnp.full((4 * DIFF_QK_WIDTH,), d_in, f32), jnp.full((DIFF_V_WIDTH,), d_in * BETA, f32),
        jnp.full((N_BRANCH * D_MODEL,), d_in, f32)])
    page_table = jax.random.permutation(ks[6], n_pool)[:n_used].reshape(DEC_BATCH, n_pages).astype(jnp.int32)
    return {
        "x_prompt": nrm(ks[0], (BATCH, SEQ, D_MODEL), 1.0),
        "x_sample": nrm(ks[1], (DEC_BATCH, DEC_SEQ, D_MODEL), 1.0),
        "cache_sb_k": nrm(ks[2], (DEPTH, n_pool, PAGE_SIZE, SB_HEADS, SB_HEAD_DIM), 1.0),
        "cache_sb_v": nrm(ks[3], (DEPTH, n_pool, PAGE_SIZE, SB_HEADS, SB_HEAD_DIM), 1.0),
        "cache_diff_k": nrm(ks[4], (DEPTH, n_pool, PAGE_SIZE, DIFF_HEADS, 2 * DIFF_HEAD_DIM), 1.0),
        "cache_diff_v": nrm(ks[5], (DEPTH, n_pool, PAGE_SIZE, DIFF_HEADS, DIFF_V_DIM), 1.0),
        "page_table": page_table,
        "rel_bias": nrm(ks[7], (N_BUCKETS, DIFF_HEADS), 0.5),
        "w_in": jax.random.normal(ks[8], (DEPTH, D_MODEL, IN_WIDTH), f32) * col_scale,
        "w_pa": nrm(ks[9], (DEPTH, SB_WIDTH, D_MODEL), SB_WIDTH ** -0.5 * BETA),
        "w_pb": nrm(ks[10], (DEPTH, DIFF_V_WIDTH, D_MODEL), DIFF_V_WIDTH ** -0.5 * BETA),
        "w_o": nrm(ks[11], (DEPTH, D_MODEL, D_MODEL), d_in * BETA),
        "diff_lambda": nrm(ks[12], (DEPTH, 4, DIFF_HEAD_DIM), 0.1),
        "diff_subln_g": 1.0 + nrm(ks[13], (DEPTH, DIFF_V_DIM), 0.02),
        "ln1_g": 1.0 + nrm(ks[14], (DEPTH, D_MODEL), 0.02),
        "ln1_b": nrm(ks[15], (DEPTH, D_MODEL), 0.02),
        "ln2_g": 1.0 + nrm(ks[16], (DEPTH, D_MODEL), 0.02),
        "ln2_b": nrm(ks[17], (DEPTH, D_MODEL), 0.02),
        "ffn_w_gate": nrm(ks[18], (N_DENSE, D_MODEL, D_FF), d_in * BETA),
        "ffn_w_up": nrm(ks[19], (N_DENSE, D_MODEL, D_FF), d_in * BETA),
        "ffn_w_down": nrm(ks[20], (N_DENSE, D_FF, D_MODEL), D_FF ** -0.5 * BETA),
        "router_w": nrm(ks[21], (N_MOE, D_MODEL, N_EXPERTS), d_in),
        "moe_w_gate": nrm(ks[22], (N_MOE, N_EXPERTS, D_MODEL, D_FF_EXPERT), d_in * BETA),
        "moe_w_up": nrm(ks[23], (N_MOE, N_EXPERTS, D_MODEL, D_FF_EXPERT), d_in * BETA),
        "moe_w_down": nrm(ks[24], (N_MOE, N_EXPERTS, D_FF_EXPERT, D_MODEL), D_FF_EXPERT ** -0.5 * BETA),
    }


def reference(x_prompt, x_sample, cache_sb_k, cache_sb_v, cache_diff_k, cache_diff_v, page_table,
              rel_bias, w_in, w_pa, w_pb, w_o, diff_lambda, diff_subln_g, ln1_g, ln1_b, ln2_g, ln2_b,
              ffn_w_gate, ffn_w_up, ffn_w_down, router_w, moe_w_gate, moe_w_up, moe_w_down):
    pos_p = jnp.arange(SEQ, dtype=jnp.int32)
    pos_s = PAST_LEN + jnp.arange(DEC_SEQ, dtype=jnp.int32)
    pos_all = jnp.arange(PAST_LEN + DEC_SEQ, dtype=jnp.int32)
    xp, xs = x_prompt, x_sample
    sbk_p, sbv_p, dk_p, dv_p = [], [], [], []
    sbk_s, sbv_s, dk_s, dv_s = [], [], [], []

    def channel_mix(x, l):
        if l % 2 == 0:
            i = l // 2
            return swiglu(x, ffn_w_gate[i], ffn_w_up[i], ffn_w_down[i])
        i = l // 2
        return moe_swiglu(x, router_w[i], moe_w_gate[i], moe_w_up[i], moe_w_down[i])

    for l in range(DEPTH):
        lam, lam_init = diff_lambda_value(diff_lambda[l], l)
        g_sub = diff_subln_g[l]

        q_sb, k_sb, v_sb, q1, q2, k1, k2, v_d, gate = project_in(xp, w_in[l])
        o_sb = sweep_query_blocks(
            lambda q, qp: sb_attend(q, k_sb, v_sb, qp, pos_p), (q_sb,), pos_p)
        o_d = sweep_query_blocks(
            lambda a, b, qp: diff_attend(a, b, k1, k2, v_d, qp, pos_p, rel_bias, lam, lam_init, g_sub),
            (q1, q2), pos_p)
        mix = merge_branches(o_sb, o_d, gate, w_pa[l], w_pb[l], w_o[l])
        xp = layer_norm(ALPHA * xp + mix, ln1_g[l], ln1_b[l])
        xp = layer_norm(ALPHA * xp + channel_mix(xp, l), ln2_g[l], ln2_b[l])
        sbk_p.append(k_sb); sbv_p.append(v_sb)
        dk_p.append(jnp.concatenate([k1, k2], axis=-1)); dv_p.append(v_d)

        sq_sb, sk_sb, sv_sb, sq1, sq2, sk1, sk2, sv_d, sgate = project_in(xs, w_in[l])
        sk_new = jnp.concatenate([sk1, sk2], axis=-1)
        k_sb_all = jnp.concatenate([gather_pages(cache_sb_k[l], page_table), sk_sb], axis=1)
        v_sb_all = jnp.concatenate([gather_pages(cache_sb_v[l], page_table), sv_sb], axis=1)
        kd_all = jnp.concatenate([gather_pages(cache_diff_k[l], page_table), sk_new], axis=1)
        vd_all = jnp.concatenate([gather_pages(cache_diff_v[l], page_table), sv_d], axis=1)
        so_sb = sb_attend(sq_sb, k_sb_all, v_sb_all, pos_s, pos_all)
        so_d = diff_attend(sq1, sq2, kd_all[..., :DIFF_HEAD_DIM], kd_all[..., DIFF_HEAD_DIM:], vd_all,
                           pos_s, pos_all, rel_bias, lam, lam_init, g_sub)
        smix = merge_branches(so_sb, so_d, sgate, w_pa[l], w_pb[l], w_o[l])
        xs = layer_norm(ALPHA * xs + smix, ln1_g[l], ln1_b[l])
        xs = layer_norm(ALPHA * xs + channel_mix(xs, l), ln2_g[l], ln2_b[l])
        sbk_s.append(sk_sb); sbv_s.append(sv_sb); dk_s.append(sk_new); dv_s.append(sv_d)

    return (xp, xs,
            jnp.stack(sbk_p), jnp.stack(sbv_p), jnp.stack(dk_p), jnp.stack(dv_p),
            jnp.stack(sbk_s), jnp.stack(sbv_s), jnp.stack(dk_s), jnp.stack(dv_s))
```

```python
import functools
import math

import jax
import jax.numpy as jnp
from jax import lax
from jax.experimental import pallas as pl
from jax.experimental.pallas import tpu as pltpu

F32 = jnp.float32
BF16 = jnp.bfloat16

PAGE_SIZE = 128
SB_HEADS = 8
SB_WIDTH = 512
DIFF_HEADS = 4
DIFF_V_DIM = 128
DIFF_QK_WIDTH = 256
DIFF_V_WIDTH = 512
N_EXPERTS = 8
N_BUCKETS = 32
MAX_DISTANCE = 128
LN_EPS = 1e-5
SUBLN_EPS = 1e-5
NEG_INF = -1e30
QK_SCALE = 0.125

LANES = 128
SUBLANES = 8
VMEM_LIMIT_BYTES = 56 * 1024 * 1024

ROW_TILE = 512
FFN_ROW_TILE = 1024
ATT_TILE = 256
DEC_PAGES = 8

NT_DIMS = (((1,), (1,)), ((), ()))


def _params(semantics):
    return pltpu.CompilerParams(dimension_semantics=semantics, vmem_limit_bytes=VMEM_LIMIT_BYTES)


def _dot(a, b):
    return jnp.dot(a, b, preferred_element_type=F32)


def _dot_nt(a, b):
    return lax.dot_general(a, b, NT_DIMS, preferred_element_type=F32)


def _layer_norm(y, g, b):
    mu = jnp.mean(y, axis=-1, keepdims=True)
    d = y - mu
    var = jnp.mean(d * d, axis=-1, keepdims=True)
    return d * lax.rsqrt(var + LN_EPS) * g + b


def _strict_lower_ones(n):
    j = lax.broadcasted_iota(jnp.int32, (n, n), 0)
    s = lax.broadcasted_iota(jnp.int32, (n, n), 1)
    return jnp.where(j > s, 1.0, 0.0).astype(BF16)


def _largest_tile(dim, cap):
    if dim <= cap:
        return dim
    best = None
    for t in range(LANES, cap + 1, LANES):
        if dim % t == 0:
            best = t
    assert best is not None, (dim, cap)
    return best


def _proj_prompt_body(x_ref, wm_ref, wkvt_ref, *rest):
    qsb_ref, q12_ref, gate_ref, k12_ref, vd_ref, ksbt_ref, vsbt_ref = rest[-7:]
    x = x_ref[...].astype(BF16)

    def cols(c0, n):
        return _dot(x, wm_ref[:, c0:c0 + n])

    qsb_ref[...] = (cols(0, 512) * QK_SCALE).astype(BF16)
    q12_ref[...] = (cols(512, 512) * QK_SCALE).astype(BF16)
    k12_ref[...] = cols(1024, 512)
    vd_ref[...] = cols(1536, 512)
    gate_ref[...] = jax.nn.sigmoid(cols(2048, 2048)).astype(BF16)
    ksbt_ref[...] = _dot_nt(wkvt_ref[0:512, :], x)
    vsbt_ref[...] = _dot_nt(wkvt_ref[512:1024, :], x)


def _proj_prompt(x, w_main, w_kvt, layer, stacked, batch, seq):
    n, d = x.shape
    depth = w_main.shape[0]
    tm = min(ROW_TILE, seq)
    nt = seq // tm
    grid = (n // tm,)
    row = lambda i: (i, 0)
    in_specs = [
        pl.BlockSpec((tm, d), row),
        pl.BlockSpec((None, d, w_main.shape[2]), lambda i: (layer, 0, 0)),
        pl.BlockSpec((None, w_kvt.shape[1], d), lambda i: (layer, 0, 0)),
    ]
    args = [x, w_main, w_kvt]
    aliases = {}
    if stacked is not None:
        in_specs += [pl.BlockSpec(memory_space=pl.ANY)] * 4
        args += list(stacked)
        aliases = {3: 3, 4: 4, 5: 5, 6: 6}
    out_shape = [
        jax.ShapeDtypeStruct((n, 512), BF16),
        jax.ShapeDtypeStruct((n, 512), BF16),
        jax.ShapeDtypeStruct((n, 2048), BF16),
        jax.ShapeDtypeStruct((depth, n, 512), F32),
        jax.ShapeDtypeStruct((depth, n, 512), F32),
        jax.ShapeDtypeStruct((depth, batch, 512, seq), F32),
        jax.ShapeDtypeStruct((depth, batch, 512, seq), F32),
    ]
    out_specs = [
        pl.BlockSpec((tm, 512), row),
        pl.BlockSpec((tm, 512), row),
        pl.BlockSpec((tm, 2048), row),
        pl.BlockSpec((None, tm, 512), lambda i: (layer, i, 0)),
        pl.BlockSpec((None, tm, 512), lambda i: (layer, i, 0)),
        pl.BlockSpec((None, None, 512, tm), lambda i: (layer, i // nt, 0, i % nt)),
        pl.BlockSpec((None, None, 512, tm), lambda i: (layer, i // nt, 0, i % nt)),
    ]
    return pl.pallas_call(
        _proj_prompt_body, grid=grid, in_specs=in_specs, out_specs=out_specs, out_shape=out_shape,
        input_output_aliases=aliases, compiler_params=_params(("arbitrary",)),
        name=f"proj_prompt_{layer}")(*args)


def _proj_sample_body(x_ref, wm_ref, wkvt_ref, qsb_ref, q12_ref, gate_ref, k12_ref, vd_ref, ksb_ref, vsb_ref):
    x = x_ref[...].astype(BF16)

    def cols(c0, n):
        return _dot(x, wm_ref[:, c0:c0 + n])

    qsb_ref[...] = (cols(0, 512) * QK_SCALE).astype(BF16)
    q12_ref[...] = (cols(512, 512) * QK_SCALE).astype(BF16)
    k12_ref[...] = cols(1024, 512)
    vd_ref[...] = cols(1536, 512)
    gate_ref[...] = jax.nn.sigmoid(cols(2048, 2048)).astype(BF16)
    ksb_ref[...] = _dot_nt(x, wkvt_ref[0:512, :])
    vsb_ref[...] = _dot_nt(x, wkvt_ref[512:1024, :])


def _proj_sample(x, w_main, w_kvt, layer):
    n, d = x.shape
    full = lambda shape: pl.BlockSpec(shape, lambda i: (0,) * len(shape))
    out_shape = [
        jax.ShapeDtypeStruct((n, 512), BF16), jax.ShapeDtypeStruct((n, 512), BF16),
        jax.ShapeDtypeStruct((n, 2048), BF16),
        jax.ShapeDtypeStruct((n, 512), F32), jax.ShapeDtypeStruct((n, 512), F32),
        jax.ShapeDtypeStruct((n, 512), F32), jax.ShapeDtypeStruct((n, 512), F32),
    ]
    return pl.pallas_call(
        _proj_sample_body, grid=(1,),
        in_specs=[full((n, d)),
                  pl.BlockSpec((None, d, w_main.shape[2]), lambda i: (layer, 0, 0)),
                  pl.BlockSpec((None, w_kvt.shape[1], d), lambda i: (layer, 0, 0))],
        out_specs=[full(s.shape) for s in out_shape], out_shape=out_shape,
        compiler_params=_params(("arbitrary",)), name=f"proj_sample_{layer}")(x, w_main, w_kvt)


def _sb_block(q, k_t, v_t, tri, r, acc, causal):
    z = _dot(q, k_t)
    l1p = jnp.log1p(jnp.exp(-jnp.abs(z)))
    log_beta = jnp.minimum(z, 0.0) - l1p
    log_rest = -(jnp.maximum(z, 0.0) + l1p)
    if causal is not None:
        log_rest = jnp.where(causal, log_rest, 0.0)
    hi = log_rest.astype(BF16)
    lo = (log_rest - hi.astype(F32)).astype(BF16)
    cum = _dot(hi, tri) + _dot(lo, tri)
    w = jnp.exp(log_beta + cum + r)
    if causal is not None:
        w = jnp.where(causal, w, 0.0)
    acc = acc + _dot_nt(w.astype(BF16), v_t)
    r = r + cum[:, 0:1] + log_rest[:, 0:1]
    return r, acc


def _sb_prompt_body(q_ref, kt_ref, vt_ref, o_ref, kt_s, vt_s, r_s, acc_s, *, tile, n_blocks):
    i = pl.program_id(2)

    @pl.when(i == 0)
    def _():
        for c in range(n_blocks):
            kt_s[c] = kt_ref[:, c * tile:(c + 1) * tile].astype(BF16)
            vt_s[c] = vt_ref[:, c * tile:(c + 1) * tile].astype(BF16)

    q2 = q_ref[...]
    lane = lax.broadcasted_iota(jnp.int32, q2.shape, 1)
    zero = jnp.zeros_like(q2)
    q_heads = (jnp.where(lane < 64, q2, zero), jnp.where(lane >= 64, q2, zero))
    row = lax.broadcasted_iota(jnp.int32, (tile, tile), 0)
    col = lax.broadcasted_iota(jnp.int32, (tile, tile), 1)
    causal = col < row
    tri = _strict_lower_ones(tile)
    r_s[...] = jnp.zeros_like(r_s)
    acc_s[...] = jnp.zeros_like(acc_s)

    def step(kb, mask):
        k_t = kt_s[kb]
        v_t = vt_s[kb]
        for h in range(2):
            r, acc = _sb_block(q_heads[h], k_t, v_t, tri, r_s[h], acc_s[h], mask)
            r_s[h] = r
            acc_s[h] = acc

    step(i, causal)

    def body(j, carry):
        step(i - 1 - j, None)
        return carry

    lax.fori_loop(0, i, body, 0)
    o_ref[...] = jnp.where(lane < 64, acc_s[0], acc_s[1]).astype(BF16)


def _sb_prompt(q, kt_all, vt_all, layer, batch, seq):
    tile = min(ATT_TILE, seq)
    nq = seq // tile
    body = functools.partial(_sb_prompt_body, tile=tile, n_blocks=nq)
    qmap = lambda b, h, i: (b * nq + i, h)
    kvmap = lambda b, h, i: (layer, b, h, 0)
    return pl.pallas_call(
        body, grid=(batch, SB_HEADS // 2, nq),
        in_specs=[pl.BlockSpec((tile, LANES), qmap),
                  pl.BlockSpec((None, None, LANES, seq), kvmap),
                  pl.BlockSpec((None, None, LANES, seq), kvmap)],
        out_specs=pl.BlockSpec((tile, LANES), qmap),
        out_shape=jax.ShapeDtypeStruct((batch * seq, SB_WIDTH), BF16),
        scratch_shapes=[pltpu.VMEM((nq, LANES, tile), BF16), pltpu.VMEM((nq, LANES, tile), BF16),
                        pltpu.VMEM((2, tile, 1), F32), pltpu.VMEM((2, tile, LANES), F32)],
        compiler_params=_params(("arbitrary", "arbitrary", "arbitrary")),
        name=f"sb_prompt_{layer}")(q, kt_all, vt_all)


def _sb_sample_body(pt_ref, q_ref, *rest, n_pages_step, n_steps):
    del pt_ref
    k_refs = rest[:n_pages_step]
    v_refs = rest[n_pages_step:2 * n_pages_step]
    o_ref, r_s, acc_s = rest[2 * n_pages_step:]
    g = pl.program_id(1)

    @pl.when(g == 0)
    def _():
        r_s[...] = jnp.zeros_like(r_s)
        acc_s[...] = jnp.zeros_like(acc_s)

    q = q_ref[...]
    row = lax.broadcasted_iota(jnp.int32, (SB_HEADS, SB_WIDTH), 0)
    col = lax.broadcasted_iota(jnp.int32, (SB_HEADS, SB_WIDTH), 1)
    own = (col >> 6) == row
    q_bd = jnp.where(own, jnp.broadcast_to(q.astype(F32), (SB_HEADS, SB_WIDTH)), 0.0).astype(BF16)
    tri = _strict_lower_ones(PAGE_SIZE)
    r = r_s[...]
    acc = acc_s[...]
    for p in reversed(range(n_pages_step)):
        k_t = k_refs[p][...].astype(BF16)
        v_t = v_refs[p][...].astype(BF16)
        r, acc = _sb_block(q_bd, k_t, v_t, tri, r, acc, None)
    r_s[...] = r
    acc_s[...] = acc

    @pl.when(g == n_steps - 1)
    def _():
        o_ref[...] = jnp.sum(jnp.where(own, acc, 0.0), axis=0, keepdims=True).astype(BF16)


def _sb_sample(q, cache_kt, cache_vt, page_table, layer):
    db = q.shape[0]
    n_pages = page_table.shape[1]
    pps = min(DEC_PAGES, n_pages)
    n_steps = n_pages // pps
    q3 = q.reshape(db, 1, SB_WIDTH)

    def page_spec(p):
        return pl.BlockSpec((None, None, SB_WIDTH, PAGE_SIZE),
                            lambda b, g, pt: (layer, pt[b, (n_steps - 1 - g) * pps + p], 0, 0))

    qspec = pl.BlockSpec((None, 1, SB_WIDTH), lambda b, g, pt: (b, 0, 0))
    grid_spec = pltpu.PrefetchScalarGridSpec(
        num_scalar_prefetch=1, grid=(db, n_steps),
        in_specs=[qspec] + [page_spec(p) for p in range(pps)] * 2,
        out_specs=qspec,
        scratch_shapes=[pltpu.VMEM((SB_HEADS, 1), F32), pltpu.VMEM((SB_HEADS, SB_WIDTH), F32)])
    body = functools.partial(_sb_sample_body, n_pages_step=pps, n_steps=n_steps)
    out = pl.pallas_call(
        body, grid_spec=grid_spec, out_shape=jax.ShapeDtypeStruct((db, 1, SB_WIDTH), BF16),
        compiler_params=_params(("arbitrary", "arbitrary")), name=f"sb_sample_{layer}")(
            page_table, q3, *([cache_kt] * pps), *([cache_vt] * pps))
    return out.reshape(db, SB_WIDTH)


def _t5_bucket(n):
    max_exact = N_BUCKETS // 2
    nf = jnp.maximum(n, 1).astype(F32)
    large = max_exact + (jnp.log(nf / max_exact) / math.log(MAX_DISTANCE / max_exact)
                         * (N_BUCKETS - max_exact)).astype(jnp.int32)
    large = jnp.minimum(large, N_BUCKETS - 1)
    return jnp.where(n < max_exact, n, large)


def _bias_lookup(bucket, head, rb_ref, n_heads_static):
    val = jnp.zeros(bucket.shape, F32)
    for k in range(N_BUCKETS):
        if isinstance(head, int):
            val = jnp.where(bucket == k, rb_ref[k, head], val)
        else:
            for hh in range(n_heads_static):
                val = jnp.where((bucket == k) & (head == hh), rb_ref[k, hh], val)
    return val


def _bias_tiles_body(rb_ref, o_ref, *, tile):
    h = pl.program_id(0)
    which = pl.program_id(1)
    row = lax.broadcasted_iota(jnp.int32, (tile, tile), 0)
    col = lax.broadcasted_iota(jnp.int32, (tile, tile), 1)
    n = jnp.maximum(row - col + which * tile, 0)
    bucket = _t5_bucket(n)
    val = jnp.zeros((tile, tile), F32)
    for k in range(N_BUCKETS):
        val = jnp.where(bucket == k, rb_ref[k, h], val)
    o_ref[...] = val


def _bias_tiles(rel_bias, tile):
    return pl.pallas_call(
        functools.partial(_bias_tiles_body, tile=tile), grid=(DIFF_HEADS, 2),
        in_specs=[pl.BlockSpec(memory_space=pltpu.SMEM)],
        out_specs=pl.BlockSpec((None, None, tile, tile), lambda h, w: (h, w, 0, 0)),
        out_shape=jax.ShapeDtypeStruct((DIFF_HEADS, 2, tile, tile), F32),
        compiler_params=_params(("arbitrary", "arbitrary")), name="bias_tiles")(rel_bias)


def _bias_sample_body(rb_ref, o_ref):
    shape = (2 * DIFF_HEADS, DIFF_HEADS * PAGE_SIZE)
    head = lax.broadcasted_iota(jnp.int32, shape, 0) & (DIFF_HEADS - 1)
    pos = lax.broadcasted_iota(jnp.int32, shape, 1) >> 2
    far = jnp.zeros(shape, jnp.int32) + MAX_DISTANCE
    last = PAGE_SIZE - pos
    new = jnp.zeros(shape, jnp.int32)
    for idx, n in enumerate((far, last, new)):
        o_ref[idx] = _bias_lookup(_t5_bucket(n), head, rb_ref, DIFF_HEADS)


def _bias_sample(rel_bias):
    shape = (3, 2 * DIFF_HEADS, DIFF_HEADS * PAGE_SIZE)
    return pl.pallas_call(
        _bias_sample_body, grid=(1,),
        in_specs=[pl.BlockSpec(memory_space=pltpu.SMEM)],
        out_specs=pl.BlockSpec(shape, lambda i: (0, 0, 0)),
        out_shape=jax.ShapeDtypeStruct(shape, F32),
        compiler_params=_params(("arbitrary",)), name="bias_sample")(rel_bias)


def _diff_lambda(lp, lam_init):
    a = jnp.sum(lp[0:1] * lp[1:2], axis=1, keepdims=True)
    b = jnp.sum(lp[2:3] * lp[3:4], axis=1, keepdims=True)
    return jnp.exp(a) - jnp.exp(b) + lam_init


def _softmax_step(s, v, m, l, acc):
    m_new = jnp.maximum(m, jnp.max(s, axis=1, keepdims=True))
    a = jnp.exp(m - m_new)
    p = jnp.exp(s - m_new)
    l = a * l + jnp.sum(p, axis=1, keepdims=True)
    acc = a * acc + _dot(p.astype(BF16), v)
    return m_new, l, acc


def _sub_layer_norm(o, g, lam_init):
    o = o * lax.rsqrt(jnp.mean(o * o, axis=-1, keepdims=True) + SUBLN_EPS)
    return o * g * (1.0 - lam_init)


def _diff_prompt_body(rb_ref, lam_ref, g_ref, q_ref, k_ref, v_ref, d_ref, o_ref,
                      k_s, v_s, m_s, l_s, acc_s, *, tile, lam_init):
    h = pl.program_id(1)
    i = pl.program_id(2)

    @pl.when(i == 0)
    def _():
        k_s[...] = k_ref[...].astype(BF16)
        v_s[...] = v_ref[...].astype(BF16)

    q2 = q_ref[...]
    lane = lax.broadcasted_iota(jnp.int32, q2.shape, 1)
    zero = jnp.zeros_like(q2)
    q_maps = (jnp.where(lane < 64, q2, zero), jnp.where(lane >= 64, q2, zero))
    row = lax.broadcasted_iota(jnp.int32, (tile, tile), 0)
    col = lax.broadcasted_iota(jnp.int32, (tile, tile), 1)
    visible = col <= row
    m_s[...] = jnp.full_like(m_s, NEG_INF)
    l_s[...] = jnp.zeros_like(l_s)
    acc_s[...] = jnp.zeros_like(acc_s)

    def step(kb, bias, mask):
        start = pl.multiple_of(kb * tile, tile)
        k = k_s[pl.ds(start, tile), :]
        v = v_s[pl.ds(start, tile), :]
        for t in range(2):
            s = _dot_nt(q_maps[t], k) + bias
            if mask is not None:
                s = jnp.where(mask, s, NEG_INF)
            m, l, acc = _softmax_step(s, v, m_s[t], l_s[t], acc_s[t])
            m_s[t] = m
            l_s[t] = l
            acc_s[t] = acc

    step(i, d_ref[0], visible)

    @pl.when(i >= 1)
    def _():
        step(i - 1, d_ref[1], None)

    far_bias = rb_ref[N_BUCKETS - 1, h]

    def body(j, carry):
        step(j, far_bias, None)
        return carry

    lax.fori_loop(0, i - 1, body, 0)
    lam = _diff_lambda(lam_ref[...], lam_init)
    o = acc_s[0] / l_s[0] - lam * (acc_s[1] / l_s[1])
    o_ref[...] = _sub_layer_norm(o, g_ref[...], lam_init).astype(BF16)


def _diff_prompt(q12, k12_all, vd_all, bias_tiles, rel_bias, lam_params, subln_g, layer, batch, seq, lam_init):
    tile = min(ATT_TILE, seq)
    assert tile >= MAX_DISTANCE
    nq = seq // tile
    body = functools.partial(_diff_prompt_body, tile=tile, lam_init=lam_init)
    qmap = lambda b, h, i: (b * nq + i, h)
    kvmap = lambda b, h, i: (layer, b, h)
    return pl.pallas_call(
        body, grid=(batch, DIFF_HEADS, nq),
        in_specs=[pl.BlockSpec(memory_space=pltpu.SMEM),
                  pl.BlockSpec((None, 4, 64), lambda b, h, i: (layer, 0, 0)),
                  pl.BlockSpec((None, 1, DIFF_V_DIM), lambda b, h, i: (layer, 0, 0)),
                  pl.BlockSpec((tile, LANES), qmap),
                  pl.BlockSpec((None, seq, LANES), kvmap),
                  pl.BlockSpec((None, seq, LANES), kvmap),
                  pl.BlockSpec((None, 2, tile, tile), lambda b, h, i: (h, 0, 0, 0))],
        out_specs=pl.BlockSpec((tile, LANES), qmap),
        out_shape=jax.ShapeDtypeStruct((batch * seq, DIFF_V_WIDTH), BF16),
        scratch_shapes=[pltpu.VMEM((seq, LANES), BF16), pltpu.VMEM((seq, LANES), BF16),
                        pltpu.VMEM((2, tile, 1), F32), pltpu.VMEM((2, tile, 1), F32),
                        pltpu.VMEM((2, tile, LANES), F32)],
        compiler_params=_params(("arbitrary", "arbitrary", "arbitrary")),
        name=f"diff_prompt_{layer}")(rel_bias, lam_params, subln_g, q12, k12_all, vd_all, bias_tiles)


def _rows_per_head(x):
    n_rows = 2 * DIFF_HEADS
    head = lax.broadcasted_iota(jnp.int32, (n_rows, LANES), 0) & (DIFF_HEADS - 1)
    out = jnp.zeros((n_rows, LANES), F32)
    for hh in range(DIFF_HEADS):
        out = jnp.where(head == hh, jnp.broadcast_to(x[:, hh * LANES:(hh + 1) * LANES], (n_rows, LANES)), out)
    return out


def _diff_sample_body(pt_ref, lam_ref, g_ref, bias_ref, q_ref, kn_ref, vn_ref, *rest,
                      n_pages_step, n_steps, lam_init):
    del pt_ref
    k_refs = rest[:n_pages_step]
    v_refs = rest[n_pages_step:2 * n_pages_step]
    o_ref, m_s, l_s, acc_s = rest[2 * n_pages_step:]
    g = pl.program_id(1)
    n_rows = 2 * DIFF_HEADS

    @pl.when(g == 0)
    def _():
        m_s[...] = jnp.full_like(m_s, NEG_INF)
        l_s[...] = jnp.zeros_like(l_s)
        acc_s[...] = jnp.zeros_like(acc_s)

    q8 = _rows_per_head(q_ref[...].astype(F32))
    r8 = lax.broadcasted_iota(jnp.int32, (n_rows, LANES), 0)
    l8 = lax.broadcasted_iota(jnp.int32, (n_rows, LANES), 1)
    q8 = jnp.where((l8 >> 6) == (r8 >> 2), q8, 0.0).astype(BF16)
    shape = (n_rows, DIFF_HEADS * PAGE_SIZE)
    row = lax.broadcasted_iota(jnp.int32, shape, 0)
    col = lax.broadcasted_iota(jnp.int32, shape, 1)
    own = (col & (DIFF_HEADS - 1)) == (row & (DIFF_HEADS - 1))
    m, l, acc = m_s[...], l_s[...], acc_s[...]
    for p in range(n_pages_step):
        k = k_refs[p][...].astype(BF16)
        v = v_refs[p][...].astype(BF16)
        is_last = jnp.logical_and(g == n_steps - 1, p == n_pages_step - 1)
        bias = jnp.where(is_last, bias_ref[1], bias_ref[0])
        s = jnp.where(own, _dot_nt(q8, k) + bias, NEG_INF)
        m, l, acc = _softmax_step(s, v, m, l, acc)
    m_s[...] = m
    l_s[...] = l
    acc_s[...] = acc

    @pl.when(g == n_steps - 1)
    def _():
        k_new = _rows_per_head(kn_ref[...]).astype(BF16).astype(F32)
        v_new = _rows_per_head(vn_ref[...]).astype(BF16).astype(F32)
        s_new = jnp.sum(q8.astype(F32) * k_new, axis=1, keepdims=True) + bias_ref[2][:, 0:1]
        m_new = jnp.maximum(m, s_new)
        a = jnp.exp(m - m_new)
        p_new = jnp.exp(s_new - m_new)
        l_f = a * l + p_new
        acc_f = a * acc + p_new.astype(BF16).astype(F32) * v_new
        d = acc_f / l_f
        lam = _diff_lambda(lam_ref[...], lam_init)
        o = d[0:DIFF_HEADS] - lam * d[DIFF_HEADS:n_rows]
        o_ref[...] = _sub_layer_norm(o, g_ref[...], lam_init).astype(BF16)


def _diff_sample(q12, k_new, v_new, cache_k, cache_v, page_table, bias_s, lam_params, subln_g, layer, lam_init):
    db = q12.shape[0]
    n_pages = page_table.shape[1]
    pps = min(DEC_PAGES, n_pages)
    n_steps = n_pages // pps
    n_rows = 2 * DIFF_HEADS
    width = DIFF_HEADS * PAGE_SIZE

    def page_spec(p):
        return pl.BlockSpec((None, None, width, LANES),
                            lambda b, g, pt: (layer, pt[b, g * pps + p], 0, 0))

    vec = pl.BlockSpec((None, 1, width), lambda b, g, pt: (b, 0, 0))
    grid_spec = pltpu.PrefetchScalarGridSpec(
        num_scalar_prefetch=1, grid=(db, n_steps),
        in_specs=[pl.BlockSpec((None, 4, 64), lambda b, g, pt: (layer, 0, 0)),
                  pl.BlockSpec((None, 1, DIFF_V_DIM), lambda b, g, pt: (layer, 0, 0)),
                  pl.BlockSpec((3, n_rows, width), lambda b, g, pt: (0, 0, 0)),
                  vec, vec, vec] + [page_spec(p) for p in range(pps)] * 2,
        out_specs=pl.BlockSpec((None, DIFF_HEADS, DIFF_V_DIM), lambda b, g, pt: (b, 0, 0)),
        scratch_shapes=[pltpu.VMEM((n_rows, 1), F32), pltpu.VMEM((n_rows, 1), F32),
                        pltpu.VMEM((n_rows, DIFF_V_DIM), F32)])
    body = functools.partial(_diff_sample_body, n_pages_step=pps, n_steps=n_steps, lam_init=lam_init)
    out = pl.pallas_call(
        body, grid_spec=grid_spec, out_shape=jax.ShapeDtypeStruct((db, DIFF_HEADS, DIFF_V_DIM), BF16),
        compiler_params=_params(("arbitrary", "arbitrary")), name=f"diff_sample_{layer}")(
            page_table, lam_params, subln_g, bias_s, q12.reshape(db, 1, width), k_new.reshape(db, 1, width),
            v_new.reshape(db, 1, width), *([cache_k] * pps), *([cache_v] * pps))
    return out.reshape(db, DIFF_V_WIDTH)


def _merge_body(*refs, alpha, with_router):
    if with_router:
        osb_ref, od_ref, gate_ref, x_ref, wpa_ref, wpb_ref, wo_ref, g_ref, b_ref, wr_ref, o_ref, gates_ref = refs
    else:
        osb_ref, od_ref, gate_ref, x_ref, wpa_ref, wpb_ref, wo_ref, g_ref, b_ref, o_ref = refs
    d = x_ref.shape[1]
    pa = _dot(osb_ref[...], wpa_ref[...])
    pb = _dot(od_ref[...], wpb_ref[...])
    mixed = gate_ref[:, 0:d].astype(F32) * pa + gate_ref[:, d:2 * d].astype(F32) * pb
    mix = _dot(mixed.astype(BF16), wo_ref[...])
    x1 = _layer_norm(alpha * x_ref[...] + mix, g_ref[...], b_ref[...])
    o_ref[...] = x1
    if with_router:
        logits = _dot(x1.astype(BF16), wr_ref[...])
        lane = lax.broadcasted_iota(jnp.int32, logits.shape, 1)
        logits = jnp.where(lane < N_EXPERTS, logits, -jnp.inf)
        lane = lane.astype(F32)
        v1 = jnp.max(logits, axis=1, keepdims=True)
        i1 = jnp.min(jnp.where(logits == v1, lane, float(LANES)), axis=1, keepdims=True)
        rest = jnp.where(lane == i1, -jnp.inf, logits)
        v2 = jnp.max(rest, axis=1, keepdims=True)
        i2 = jnp.min(jnp.where(rest == v2, lane, float(LANES)), axis=1, keepdims=True)
        e = jnp.exp(v2 - v1)
        w1 = 1.0 / (1.0 + e)
        w2 = e / (1.0 + e)
        gates_ref[...] = jnp.where(lane == i1, w1, 0.0) + jnp.where(lane == i2, w2, 0.0)


def _merge(o_sb, o_d, gate, x, w_pa, w_pb, w_o, ln_g, ln_b, layer, alpha, router_w=None, moe_index=None):
    n, d = x.shape
    tm = min(ROW_TILE, n)
    row = lambda i: (i, 0)
    wsel = lambda i: (layer, 0, 0)
    vsel = lambda i: (layer, 0, 0)
    in_specs = [pl.BlockSpec((tm, SB_WIDTH), row), pl.BlockSpec((tm, DIFF_V_WIDTH), row),
                pl.BlockSpec((tm, 2 * d), row), pl.BlockSpec((tm, d), row),
                pl.BlockSpec((None, SB_WIDTH, d), wsel), pl.BlockSpec((None, DIFF_V_WIDTH, d), wsel),
                pl.BlockSpec((None, d, d), wsel), pl.BlockSpec((None, 1, d), vsel), pl.BlockSpec((None, 1, d), vsel)]
    args = [o_sb, o_d, gate, x, w_pa, w_pb, w_o, ln_g, ln_b]
    out_shape = [jax.ShapeDtypeStruct((n, d), F32)]
    out_specs = [pl.BlockSpec((tm, d), row)]
    with_router = router_w is not None
    if with_router:
        in_specs.append(pl.BlockSpec((None, d, LANES), lambda i: (moe_index, 0, 0)))
        args.append(router_w)
        out_shape.append(jax.ShapeDtypeStruct((n, LANES), F32))
        out_specs.append(pl.BlockSpec((tm, LANES), row))
    body = functools.partial(_merge_body, alpha=alpha, with_router=with_router)
    res = pl.pallas_call(
        body, grid=(n // tm,), in_specs=in_specs, out_specs=out_specs, out_shape=out_shape,
        compiler_params=_params(("arbitrary",)), name=f"merge_{layer}_{n}")(*args)
    return (res[0], res[1]) if with_router else (res[0], None)


def _swiglu_hidden(xb, wg, wu):
    hg = _dot(xb, wg)
    hu = _dot(xb, wu)
    return (hg * jax.nn.sigmoid(hg) * hu).astype(BF16)


def _ffn_body(x_ref, wg_ref, wu_ref, wd_ref, g_ref, b_ref, o_ref, acc_s, xb_s, *, alpha, n_ff):
    j = pl.program_id(1)

    @pl.when(j == 0)
    def _():
        acc_s[...] = jnp.zeros_like(acc_s)
        xb_s[...] = x_ref[...].astype(BF16)

    h = _swiglu_hidden(xb_s[...], wg_ref[...], wu_ref[...])
    acc_s[...] += _dot(h, wd_ref[...])

    @pl.when(j == n_ff - 1)
    def _():
        o_ref[...] = _layer_norm(alpha * x_ref[...] + acc_s[...], g_ref[...], b_ref[...])


def _ffn_dense(x, wg, wu, wd, ln_g, ln_b, layer, index, alpha):
    n, d = x.shape
    f = wg.shape[2]
    tm = min(FFN_ROW_TILE, n)
    tf = _largest_tile(f, 256)
    n_ff = f // tf
    body = functools.partial(_ffn_body, alpha=alpha, n_ff=n_ff)
    return pl.pallas_call(
        body, grid=(n // tm, n_ff),
        in_specs=[pl.BlockSpec((tm, d), lambda i, j: (i, 0)),
                  pl.BlockSpec((None, d, tf), lambda i, j: (index, 0, j)),
                  pl.BlockSpec((None, d, tf), lambda i, j: (index, 0, j)),
                  pl.BlockSpec((None, tf, d), lambda i, j: (index, j, 0)),
                  pl.BlockSpec((None, 1, d), lambda i, j: (layer, 0, 0)),
                  pl.BlockSpec((None, 1, d), lambda i, j: (layer, 0, 0))],
        out_specs=pl.BlockSpec((tm, d), lambda i, j: (i, 0)),
        out_shape=jax.ShapeDtypeStruct((n, d), F32),
        scratch_shapes=[pltpu.VMEM((tm, d), F32), pltpu.VMEM((tm, d), BF16)],
        compiler_params=_params(("arbitrary", "arbitrary")), name=f"ffn_{layer}_{n}")(x, wg, wu, wd, ln_g, ln_b)


def _moe_body(x_ref, gates_ref, wg_ref, wu_ref, wd_ref, g_ref, b_ref, o_ref, acc_s, xb_s, *, alpha, n_ff):
    e = pl.program_id(1)
    j = pl.program_id(2)

    @pl.when(jnp.logical_and(e == 0, j == 0))
    def _():
        acc_s[...] = jnp.zeros_like(acc_s)
        xb_s[...] = x_ref[...].astype(BF16)

    h = _swiglu_hidden(xb_s[...], wg_ref[...].astype(BF16), wu_ref[...].astype(BF16))
    y = _dot(h, wd_ref[...].astype(BF16))
    gates = gates_ref[...]
    lane = lax.broadcasted_iota(jnp.int32, gates.shape, 1)
    gate_e = jnp.sum(jnp.where(lane == e, gates, 0.0), axis=1, keepdims=True)
    acc_s[...] += gate_e * y

    @pl.when(jnp.logical_and(e == N_EXPERTS - 1, j == n_ff - 1))
    def _():
        o_ref[...] = _layer_norm(alpha * x_ref[...] + acc_s[...], g_ref[...], b_ref[...])


def _ffn_moe(x, gates, wg, wu, wd, ln_g, ln_b, layer, index, alpha):
    n, d = x.shape
    f = wg.shape[3]
    tm = min(FFN_ROW_TILE, n)
    tf = _largest_tile(f, 256)
    n_ff = f // tf
    body = functools.partial(_moe_body, alpha=alpha, n_ff=n_ff)
    return pl.pallas_call(
        body, grid=(n // tm, N_EXPERTS, n_ff),
        in_specs=[pl.BlockSpec((tm, d), lambda i, e, j: (i, 0)),
                  pl.BlockSpec((tm, LANES), lambda i, e, j: (i, 0)),
                  pl.BlockSpec((None, None, d, tf), lambda i, e, j: (index, e, 0, j)),
                  pl.BlockSpec((None, None, d, tf), lambda i, e, j: (index, e, 0, j)),
                  pl.BlockSpec((None, None, tf, d), lambda i, e, j: (index, e, j, 0)),
                  pl.BlockSpec((None, 1, d), lambda i, e, j: (layer, 0, 0)),
                  pl.BlockSpec((None, 1, d), lambda i, e, j: (layer, 0, 0))],
        out_specs=pl.BlockSpec((tm, d), lambda i, e, j: (i, 0)),
        out_shape=jax.ShapeDtypeStruct((n, d), F32),
        scratch_shapes=[pltpu.VMEM((tm, d), F32), pltpu.VMEM((tm, d), BF16)],
        compiler_params=_params(("arbitrary", "arbitrary", "arbitrary")),
        name=f"moe_{layer}_{n}")(x, gates, wg, wu, wd, ln_g, ln_b)


def _prepare_in_proj(w_in):
    depth, d, _ = w_in.shape

    def pair(a, b):
        a = a.reshape(depth, d, DIFF_HEADS, 64)
        b = b.reshape(depth, d, DIFF_HEADS, 64)
        return jnp.concatenate([a, b], axis=-1).reshape(depth, d, 2 * DIFF_QK_WIDTH)

    c = [0, 512, 1024, 1536, 1792, 2048, 2304, 2560, 3072, 3072 + 2 * d]
    part = [w_in[:, :, c[k]:c[k + 1]] for k in range(9)]
    q_sb, k_sb, v_sb, q1, q2, k1, k2, v_d, gate = part
    w_main = jnp.concatenate([q_sb, pair(q1, q2), pair(k1, k2), v_d, gate], axis=-1).astype(BF16)
    w_kvt = jnp.swapaxes(jnp.concatenate([k_sb, v_sb], axis=-1), 1, 2).astype(BF16)
    return w_main, w_kvt


def kernel(x_prompt, x_sample, cache_sb_k, cache_sb_v, cache_diff_k, cache_diff_v, page_table, rel_bias, w_in,
           w_pa, w_pb, w_o, diff_lambda, diff_subln_g, ln1_g, ln1_b, ln2_g, ln2_b, ffn_w_gate, ffn_w_up,
           ffn_w_down, router_w, moe_w_gate, moe_w_up, moe_w_down):
    batch, seq, d = x_prompt.shape
    db = x_sample.shape[0]
    depth = w_in.shape[0]
    n_pool = cache_sb_k.shape[1]
    alpha = (2.0 * depth) ** 0.25
    n = batch * seq

    w_main, w_kvt = _prepare_in_proj(w_in)
    w_pa_b, w_pb_b, w_o_b = w_pa.astype(BF16), w_pb.astype(BF16), w_o.astype(BF16)
    ffn_g_b, ffn_u_b, ffn_d_b = ffn_w_gate.astype(BF16), ffn_w_up.astype(BF16), ffn_w_down.astype(BF16)
    router_b = jnp.pad(router_w, ((0, 0), (0, 0), (0, LANES - N_EXPERTS))).astype(BF16)

    sbk_t = jnp.transpose(cache_sb_k, (0, 1, 3, 4, 2)).reshape(depth, n_pool, SB_WIDTH, PAGE_SIZE)
    sbv_t = jnp.transpose(cache_sb_v, (0, 1, 3, 4, 2)).reshape(depth, n_pool, SB_WIDTH, PAGE_SIZE)
    dk_r = cache_diff_k.reshape(depth, n_pool, PAGE_SIZE * DIFF_HEADS, LANES)
    dv_r = cache_diff_v.reshape(depth, n_pool, PAGE_SIZE * DIFF_HEADS, LANES)

    ln1_g, ln1_b, ln2_g, ln2_b = (v.reshape(depth, 1, d) for v in (ln1_g, ln1_b, ln2_g, ln2_b))
    diff_subln_g = diff_subln_g.reshape(depth, 1, DIFF_V_DIM)

    bias_t = _bias_tiles(rel_bias, min(ATT_TILE, seq))
    bias_s = _bias_sample(rel_bias)

    xp = x_prompt.reshape(n, d)
    xs = x_sample.reshape(db, d)
    stacked = None
    s_ksb, s_vsb, s_k12, s_vd = [], [], [], []
    for l in range(depth):
        lam_init = 0.8 - 0.6 * math.exp(-0.3 * l)
        is_moe = l % 2 == 1
        idx = l // 2

        q_sb, q12, gate, k12_all, vd_all, kt_all, vt_all = _proj_prompt(xp, w_main, w_kvt, l, stacked, batch, seq)
        stacked = (k12_all, vd_all, kt_all, vt_all)
        o_sb = _sb_prompt(q_sb, kt_all, vt_all, l, batch, seq)
        o_d = _diff_prompt(q12, k12_all, vd_all, bias_t, rel_bias, diff_lambda, diff_subln_g, l, batch, seq, lam_init)
        xp, gates = _merge(o_sb, o_d, gate, xp, w_pa_b, w_pb_b, w_o_b, ln1_g, ln1_b, l, alpha,
                           router_b if is_moe else None, idx)
        if is_moe:
            xp = _ffn_moe(xp, gates, moe_w_gate, moe_w_up, moe_w_down, ln2_g, ln2_b, l, idx, alpha)
        else:
            xp = _ffn_dense(xp, ffn_g_b, ffn_u_b, ffn_d_b, ln2_g, ln2_b, l, idx, alpha)

        sq_sb, sq12, sgate, sk12, svd, sksb, svsb = _proj_sample(xs, w_main, w_kvt, l)
        so_sb = _sb_sample(sq_sb, sbk_t, sbv_t, page_table, l)
        so_d = _diff_sample(sq12, sk12, svd, dk_r, dv_r, page_table, bias_s, diff_lambda, diff_subln_g, l, lam_init)
        xs, sgates = _merge(so_sb, so_d, sgate, xs, w_pa_b, w_pb_b, w_o_b, ln1_g, ln1_b, l, alpha,
                            router_b if is_moe else None, idx)
        if is_moe:
            xs = _ffn_moe(xs, sgates, moe_w_gate, moe_w_up, moe_w_down, ln2_g, ln2_b, l, idx, alpha)
        else:
            xs = _ffn_dense(xs, ffn_g_b, ffn_u_b, ffn_d_b, ln2_g, ln2_b, l, idx, alpha)
        s_ksb.append(sksb)
        s_vsb.append(svsb)
        s_k12.append(sk12)
        s_vd.append(svd)

    k12_all, vd_all, kt_all, vt_all = stacked
    to_sb = lambda t: jnp.transpose(t.reshape(depth, batch, SB_HEADS, 64, seq), (0, 1, 4, 2, 3))
    to_diff = lambda t: t.reshape(depth, batch, seq, DIFF_HEADS, DIFF_V_DIM)
    return (xp.reshape(batch, seq, d), xs.reshape(db, 1, d),
            to_sb(kt_all), to_sb(vt_all), to_diff(k12_all), to_diff(vd_all),
            jnp.stack(s_ksb).reshape(depth, db, 1, SB_HEADS, 64),
            jnp.stack(s_vsb).reshape(depth, db, 1, SB_HEADS, 64),
            jnp.stack(s_k12).reshape(depth, db, 1, DIFF_HEADS, DIFF_V_DIM),
            jnp.stack(s_vd).reshape(depth, db, 1, DIFF_HEADS, DIFF_V_DIM))
```

```python
import functools
import math

import jax
import jax.numpy as jnp
from jax import lax
from jax.experimental import pallas as pl
from jax.experimental.pallas import tpu as pltpu

F32 = jnp.float32
BF16 = jnp.bfloat16

PAGE_SIZE = 128
SB_HEADS = 8
SB_WIDTH = 512
DIFF_HEADS = 4
DIFF_V_DIM = 128
DIFF_QK_WIDTH = 256
DIFF_V_WIDTH = 512
N_EXPERTS = 8
N_BUCKETS = 32
MAX_DISTANCE = 128
LN_EPS = 1e-5
SUBLN_EPS = 1e-5
NEG_INF = -1e30
QK_SCALE = 0.125

LANES = 128
SUBLANES = 8
VMEM_LIMIT_BYTES = 56 * 1024 * 1024

ROW_TILE = 512
FFN_ROW_TILE = 1024
DIFF_TILE = 512
SB_TILE_Q = 512
SB_TILE_K = 512
DEC_PAGES = 8
SB_HEADS_PER_STEP = 4
DIFF_HEADS_PER_STEP = 2

NT_DIMS = (((1,), (1,)), ((), ()))


def _params(semantics):
    return pltpu.CompilerParams(dimension_semantics=semantics, vmem_limit_bytes=VMEM_LIMIT_BYTES)


def _dot(a, b):
    return jnp.dot(a, b, preferred_element_type=F32)


def _dot_nt(a, b):
    return lax.dot_general(a, b, NT_DIMS, preferred_element_type=F32)


def _layer_norm(y, g, b):
    mu = jnp.mean(y, axis=-1, keepdims=True)
    d = y - mu
    var = jnp.mean(d * d, axis=-1, keepdims=True)
    return d * lax.rsqrt(var + LN_EPS) * g + b


def _strict_lower_ones(n):
    j = lax.broadcasted_iota(jnp.int32, (n, n), 0)
    s = lax.broadcasted_iota(jnp.int32, (n, n), 1)
    return jnp.where(j > s, 1.0, 0.0).astype(BF16)


def _largest_tile(dim, cap):
    if dim <= cap:
        return dim
    best = None
    for t in range(LANES, cap + 1, LANES):
        if dim % t == 0:
            best = t
    assert best is not None, (dim, cap)
    return best


def _proj_prompt_body(x_ref, wm_ref, wkvt_ref, *rest):
    qsb_ref, q12_ref, gate_ref, k12_ref, vd_ref, ksbt_ref, vsbt_ref = rest[-7:]
    x = x_ref[...].astype(BF16)

    def cols(c0, n):
        return _dot(x, wm_ref[:, c0:c0 + n])

    qsb_ref[...] = (cols(0, 512) * QK_SCALE).astype(BF16)
    q12_ref[...] = (cols(512, 512) * QK_SCALE).astype(BF16)
    k12_ref[...] = cols(1024, 512)
    vd_ref[...] = cols(1536, 512)
    gate_ref[...] = jax.nn.sigmoid(cols(2048, 2048)).astype(BF16)
    ksbt_ref[...] = _dot_nt(wkvt_ref[0:512, :], x)
    vsbt_ref[...] = _dot_nt(wkvt_ref[512:1024, :], x)


def _proj_prompt(x, w_main, w_kvt, layer, stacked, batch, seq):
    n, d = x.shape
    depth = w_main.shape[0]
    tm = min(ROW_TILE, seq)
    nt = seq // tm
    grid = (n // tm,)
    row = lambda i: (i, 0)
    in_specs = [
        pl.BlockSpec((tm, d), row),
        pl.BlockSpec((None, d, w_main.shape[2]), lambda i: (layer, 0, 0)),
        pl.BlockSpec((None, w_kvt.shape[1], d), lambda i: (layer, 0, 0)),
    ]
    args = [x, w_main, w_kvt]
    aliases = {}
    if stacked is not None:
        in_specs += [pl.BlockSpec(memory_space=pl.ANY)] * 4
        args += list(stacked)
        aliases = {3: 3, 4: 4, 5: 5, 6: 6}
    out_shape = [
        jax.ShapeDtypeStruct((n, 512), BF16),
        jax.ShapeDtypeStruct((n, 512), BF16),
        jax.ShapeDtypeStruct((n, 2048), BF16),
        jax.ShapeDtypeStruct((depth, n, 512), F32),
        jax.ShapeDtypeStruct((depth, n, 512), F32),
        jax.ShapeDtypeStruct((depth, batch, 512, seq), F32),
        jax.ShapeDtypeStruct((depth, batch, 512, seq), F32),
    ]
    out_specs = [
        pl.BlockSpec((tm, 512), row),
        pl.BlockSpec((tm, 512), row),
        pl.BlockSpec((tm, 2048), row),
        pl.BlockSpec((None, tm, 512), lambda i: (layer, i, 0)),
        pl.BlockSpec((None, tm, 512), lambda i: (layer, i, 0)),
        pl.BlockSpec((None, None, 512, tm), lambda i: (layer, i // nt, 0, i % nt)),
        pl.BlockSpec((None, None, 512, tm), lambda i: (layer, i // nt, 0, i % nt)),
    ]
    return pl.pallas_call(
        _proj_prompt_body, grid=grid, in_specs=in_specs, out_specs=out_specs, out_shape=out_shape,
        input_output_aliases=aliases, compiler_params=_params(("arbitrary",)),
        name=f"proj_prompt_{layer}")(*args)


def _proj_sample_body(x_ref, wm_ref, wkvt_ref, qsb_ref, q12_ref, gate_ref, k12_ref, vd_ref, ksb_ref, vsb_ref):
    x = x_ref[...].astype(BF16)

    def cols(c0, n):
        return _dot(x, wm_ref[:, c0:c0 + n])

    qsb_ref[...] = (cols(0, 512) * QK_SCALE).astype(BF16)
    q12_ref[...] = (cols(512, 512) * QK_SCALE).astype(BF16)
    k12_ref[...] = cols(1024, 512)
    vd_ref[...] = cols(1536, 512)
    gate_ref[...] = jax.nn.sigmoid(cols(2048, 2048)).astype(BF16)
    ksb_ref[...] = _dot_nt(x, wkvt_ref[0:512, :])
    vsb_ref[...] = _dot_nt(x, wkvt_ref[512:1024, :])


def _proj_sample(x, w_main, w_kvt, layer):
    n, d = x.shape
    full = lambda shape: pl.BlockSpec(shape, lambda i: (0,) * len(shape))
    out_shape = [
        jax.ShapeDtypeStruct((n, 512), BF16), jax.ShapeDtypeStruct((n, 512), BF16),
        jax.ShapeDtypeStruct((n, 2048), BF16),
        jax.ShapeDtypeStruct((n, 512), F32), jax.ShapeDtypeStruct((n, 512), F32),
        jax.ShapeDtypeStruct((n, 512), F32), jax.ShapeDtypeStruct((n, 512), F32),
    ]
    return pl.pallas_call(
        _proj_sample_body, grid=(1,),
        in_specs=[full((n, d)),
                  pl.BlockSpec((None, d, w_main.shape[2]), lambda i: (layer, 0, 0)),
                  pl.BlockSpec((None, w_kvt.shape[1], d), lambda i: (layer, 0, 0))],
        out_specs=[full(s.shape) for s in out_shape], out_shape=out_shape,
        compiler_params=_params(("arbitrary",)), name=f"proj_sample_{layer}")(x, w_main, w_kvt)


def _softplus(z):
    return jnp.maximum(z, 0.0) + jnp.log(1.0 + jnp.exp(-jnp.abs(z)))


def _neg_suffix_sums(sp, neg_tri):
    hi = sp.astype(BF16)
    lo = (sp - hi.astype(F32)).astype(BF16)
    return _dot(hi, neg_tri) + _dot(lo, neg_tri)


def _sb_block(q, k_t, v_t, neg_tri, r, acc, causal):
    z = _dot(q, k_t)
    sp = _softplus(z)
    log_beta = z - sp
    if causal is not None:
        sp = jnp.where(causal, sp, 0.0)
    cum = _neg_suffix_sums(sp, neg_tri)
    w = jnp.exp(log_beta + cum + r)
    if causal is not None:
        w = jnp.where(causal, w, 0.0)
    acc = acc + _dot_nt(w.astype(BF16), v_t)
    r = r + cum[:, 0:1] - sp[:, 0:1]
    return r, acc


def _sb_prompt_body(q_ref, kt_ref, vt_ref, o_ref, kt_s, vt_s, r_s, acc_s, *, tq, tk, n_blocks, n_pairs):
    i = pl.program_id(2)
    ratio = tq // tk

    @pl.when(i == 0)
    def _():
        for c in range(n_blocks):
            kt_s[c] = kt_ref[:, c * tk:(c + 1) * tk].astype(BF16)
            vt_s[c] = vt_ref[:, c * tk:(c + 1) * tk].astype(BF16)

    lane = lax.broadcasted_iota(jnp.int32, (tq, LANES), 1)
    q_heads = []
    for hp in range(n_pairs):
        q2 = q_ref[:, hp * LANES:(hp + 1) * LANES]
        zero = jnp.zeros_like(q2)
        q_heads += [jnp.where(lane < 64, q2, zero), jnp.where(lane >= 64, q2, zero)]
    row = lax.broadcasted_iota(jnp.int32, (tq, tk), 0)
    col = lax.broadcasted_iota(jnp.int32, (tq, tk), 1)
    neg_tri = -_strict_lower_ones(tk)
    r_s[...] = jnp.zeros_like(r_s)
    acc_s[...] = jnp.zeros_like(acc_s)

    def step(kb, mask):
        for h in range(2 * n_pairs):
            rows = slice((h // 2) * LANES, (h // 2 + 1) * LANES)
            r, acc = _sb_block(q_heads[h], kt_s[kb, rows, :], vt_s[kb, rows, :], neg_tri, r_s[h], acc_s[h], mask)
            r_s[h] = r
            acc_s[h] = acc

    for d in reversed(range(ratio)):
        step(i * ratio + d, col + d * tk < row)

    def body(j, carry):
        step(i * ratio - 1 - j, None)
        return carry

    lax.fori_loop(0, i * ratio, body, 0)
    for hp in range(n_pairs):
        o_ref[:, hp * LANES:(hp + 1) * LANES] = jnp.where(lane < 64, acc_s[2 * hp], acc_s[2 * hp + 1]).astype(BF16)


def _sb_prompt(q, kt_all, vt_all, layer, batch, seq):
    tq = min(SB_TILE_Q, seq)
    tk = min(SB_TILE_K, tq)
    nq = seq // tq
    n_pairs = SB_HEADS_PER_STEP // 2
    width = n_pairs * LANES
    body = functools.partial(_sb_prompt_body, tq=tq, tk=tk, n_blocks=seq // tk, n_pairs=n_pairs)
    qmap = lambda b, h, i: (b * nq + i, h)
    kvmap = lambda b, h, i: (layer, b, h, 0)
    return pl.pallas_call(
        body, grid=(batch, SB_HEADS // SB_HEADS_PER_STEP, nq),
        in_specs=[pl.BlockSpec((tq, width), qmap),
                  pl.BlockSpec((None, None, width, seq), kvmap),
                  pl.BlockSpec((None, None, width, seq), kvmap)],
        out_specs=pl.BlockSpec((tq, width), qmap),
        out_shape=jax.ShapeDtypeStruct((batch * seq, SB_WIDTH), BF16),
        scratch_shapes=[pltpu.VMEM((seq // tk, width, tk), BF16), pltpu.VMEM((seq // tk, width, tk), BF16),
                        pltpu.VMEM((SB_HEADS_PER_STEP, tq, 1), F32),
                        pltpu.VMEM((SB_HEADS_PER_STEP, tq, LANES), F32)],
        compiler_params=_params(("arbitrary", "arbitrary", "arbitrary")),
        name=f"sb_prompt_{layer}")(q, kt_all, vt_all)


def _sb_sample_body(pt_ref, q_ref, *rest, n_pages_step, n_steps):
    del pt_ref
    k_refs = rest[:n_pages_step]
    v_refs = rest[n_pages_step:2 * n_pages_step]
    o_ref, kt_s, vt_s, r_s, acc_s = rest[2 * n_pages_step:]
    g = pl.program_id(1)
    page = lambda p: slice(p * PAGE_SIZE, (p + 1) * PAGE_SIZE)

    @pl.when(g == 0)
    def _():
        r_s[...] = jnp.zeros_like(r_s)
        acc_s[...] = jnp.zeros_like(acc_s)

    for p in range(n_pages_step):
        kt_s[:, page(p)] = k_refs[p][...].astype(BF16)
        vt_s[:, page(p)] = v_refs[p][...].astype(BF16)

    q = q_ref[...]
    row = lax.broadcasted_iota(jnp.int32, (SB_HEADS, SB_WIDTH), 0)
    col = lax.broadcasted_iota(jnp.int32, (SB_HEADS, SB_WIDTH), 1)
    own = (col >> 6) == row
    q_bd = jnp.where(own, jnp.broadcast_to(q.astype(F32), (SB_HEADS, SB_WIDTH)), 0.0).astype(BF16)
    z = _dot(q_bd, kt_s[...])
    sp = _softplus(z)
    log_beta = z - sp
    sp_rows = jnp.concatenate([sp[:, page(p)] for p in range(n_pages_step)], axis=0)
    cum_rows = _neg_suffix_sums(sp_rows, -_strict_lower_ones(PAGE_SIZE))
    r = r_s[...]
    w = [None] * n_pages_step
    for p in reversed(range(n_pages_step)):
        cum = cum_rows[p * SB_HEADS:(p + 1) * SB_HEADS]
        sp_p = sp[:, page(p)]
        w[p] = jnp.exp(log_beta[:, page(p)] + cum + r)
        r = r + jnp.broadcast_to(cum[:, 0:1] - sp_p[:, 0:1], r.shape)
    r_s[...] = r
    acc = acc_s[...] + _dot_nt(jnp.concatenate(w, axis=1).astype(BF16), vt_s[...])
    acc_s[...] = acc

    @pl.when(g == n_steps - 1)
    def _():
        o_ref[...] = jnp.sum(jnp.where(own, acc, 0.0), axis=0, keepdims=True).astype(BF16)


def _sb_sample(q, cache_kt, cache_vt, page_table, layer):
    db = q.shape[0]
    n_pages = page_table.shape[1]
    pps = min(DEC_PAGES, n_pages)
    n_steps = n_pages // pps
    q3 = q.reshape(db, 1, SB_WIDTH)

    def page_spec(p):
        return pl.BlockSpec((None, None, SB_WIDTH, PAGE_SIZE),
                            lambda b, g, pt: (layer, pt[b, (n_steps - 1 - g) * pps + p], 0, 0))

    qspec = pl.BlockSpec((None, 1, SB_WIDTH), lambda b, g, pt: (b, 0, 0))
    grid_spec = pltpu.PrefetchScalarGridSpec(
        num_scalar_prefetch=1, grid=(db, n_steps),
        in_specs=[qspec] + [page_spec(p) for p in range(pps)] * 2,
        out_specs=qspec,
        scratch_shapes=[pltpu.VMEM((SB_WIDTH, pps * PAGE_SIZE), BF16), pltpu.VMEM((SB_WIDTH, pps * PAGE_SIZE), BF16),
                        pltpu.VMEM((SB_HEADS, LANES), F32), pltpu.VMEM((SB_HEADS, SB_WIDTH), F32)])
    body = functools.partial(_sb_sample_body, n_pages_step=pps, n_steps=n_steps)
    out = pl.pallas_call(
        body, grid_spec=grid_spec, out_shape=jax.ShapeDtypeStruct((db, 1, SB_WIDTH), BF16),
        compiler_params=_params(("arbitrary", "arbitrary")), name=f"sb_sample_{layer}")(
            page_table, q3, *([cache_kt] * pps), *([cache_vt] * pps))
    return out.reshape(db, SB_WIDTH)


def _t5_bucket(n):
    max_exact = N_BUCKETS // 2
    nf = jnp.maximum(n, 1).astype(F32)
    large = max_exact + (jnp.log(nf / max_exact) / math.log(MAX_DISTANCE / max_exact)
                         * (N_BUCKETS - max_exact)).astype(jnp.int32)
    large = jnp.minimum(large, N_BUCKETS - 1)
    return jnp.where(n < max_exact, n, large)


def _bias_lookup(bucket, head, rb_ref, n_heads_static):
    val = jnp.zeros(bucket.shape, F32)
    for k in range(N_BUCKETS):
        if isinstance(head, int):
            val = jnp.where(bucket == k, rb_ref[k, head], val)
        else:
            for hh in range(n_heads_static):
                val = jnp.where((bucket == k) & (head == hh), rb_ref[k, hh], val)
    return val


def _bias_tiles_body(rb_ref, o_ref, *, tile):
    h = pl.program_id(0)
    which = pl.program_id(1)
    row = lax.broadcasted_iota(jnp.int32, (tile, tile), 0)
    col = lax.broadcasted_iota(jnp.int32, (tile, tile), 1)
    n = jnp.maximum(row - col + which * tile, 0)
    bucket = _t5_bucket(n)
    val = jnp.zeros((tile, tile), F32)
    for k in range(N_BUCKETS):
        val = jnp.where(bucket == k, rb_ref[k, h], val)
    o_ref[...] = val - rb_ref[N_BUCKETS - 1, h]


def _bias_tiles(rel_bias, tile):
    return pl.pallas_call(
        functools.partial(_bias_tiles_body, tile=tile), grid=(DIFF_HEADS, 2),
        in_specs=[pl.BlockSpec(memory_space=pltpu.SMEM)],
        out_specs=pl.BlockSpec((None, None, tile, tile), lambda h, w: (h, w, 0, 0)),
        out_shape=jax.ShapeDtypeStruct((DIFF_HEADS, 2, tile, tile), F32),
        compiler_params=_params(("arbitrary", "arbitrary")), name="bias_tiles")(rel_bias)


def _bias_sample_body(rb_ref, o_ref):
    shape = (2 * DIFF_HEADS, DIFF_HEADS * PAGE_SIZE)
    head = lax.broadcasted_iota(jnp.int32, shape, 0) & (DIFF_HEADS - 1)
    pos = lax.broadcasted_iota(jnp.int32, shape, 1) >> 2
    far = jnp.zeros(shape, jnp.int32) + MAX_DISTANCE
    last = PAGE_SIZE - pos
    new = jnp.zeros(shape, jnp.int32)
    far_bias = _bias_lookup(_t5_bucket(far), head, rb_ref, DIFF_HEADS)
    for idx, n in enumerate((last, new)):
        o_ref[idx] = _bias_lookup(_t5_bucket(n), head, rb_ref, DIFF_HEADS) - far_bias


def _bias_sample(rel_bias):
    shape = (2, 2 * DIFF_HEADS, DIFF_HEADS * PAGE_SIZE)
    return pl.pallas_call(
        _bias_sample_body, grid=(1,),
        in_specs=[pl.BlockSpec(memory_space=pltpu.SMEM)],
        out_specs=pl.BlockSpec(shape, lambda i: (0, 0, 0)),
        out_shape=jax.ShapeDtypeStruct(shape, F32),
        compiler_params=_params(("arbitrary",)), name="bias_sample")(rel_bias)


def _diff_lambda(lp, lam_init):
    a = jnp.sum(lp[0:1] * lp[1:2], axis=1, keepdims=True)
    b = jnp.sum(lp[2:3] * lp[3:4], axis=1, keepdims=True)
    return jnp.exp(a) - jnp.exp(b) + lam_init


def _softmax_step(s, v, m, l, acc):
    m_new = jnp.maximum(m, jnp.max(s, axis=1, keepdims=True))
    a = jnp.exp(m - m_new)
    p = jnp.exp(s - m_new)
    l = a * l + jnp.sum(p, axis=1, keepdims=True)
    acc = a * acc + _dot(p.astype(BF16), v)
    return m_new, l, acc


def _sub_layer_norm(o, g, lam_init):
    o = o * lax.rsqrt(jnp.mean(o * o, axis=-1, keepdims=True) + SUBLN_EPS)
    return o * g * (1.0 - lam_init)


def _diff_prompt_body(lam_ref, g_ref, q_ref, k_ref, v_ref, d_ref, o_ref,
                      k_s, v_s, m_s, acc_s, *, tile, lam_init, n_heads):
    i = pl.program_id(2)
    head = lambda hd: slice(hd * LANES, (hd + 1) * LANES)

    @pl.when(i == 0)
    def _():
        k_s[...] = k_ref[...].astype(BF16)
        for hd in range(n_heads):
            v_s[hd, :, 0:LANES] = v_ref[:, head(hd)].astype(BF16)
            v_s[hd, :, LANES:2 * LANES] = jnp.ones((v_s.shape[1], LANES), BF16)

    lane = lax.broadcasted_iota(jnp.int32, (tile, LANES), 1)
    q_maps = []
    for hd in range(n_heads):
        q2 = q_ref[:, head(hd)]
        zero = jnp.zeros_like(q2)
        q_maps += [jnp.where(lane < 64, q2, zero), jnp.where(lane >= 64, q2, zero)]
    row = lax.broadcasted_iota(jnp.int32, (tile, tile), 0)
    col = lax.broadcasted_iota(jnp.int32, (tile, tile), 1)
    visible = col <= row
    m_s[...] = jnp.full_like(m_s, NEG_INF)
    acc_s[...] = jnp.zeros_like(acc_s)

    def step(kb, near, mask):
        start = pl.multiple_of(kb * tile, tile)
        for c in range(2 * n_heads):
            hd = c // 2
            s = _dot_nt(q_maps[c], k_s[pl.ds(start, tile), head(hd)])
            if near is not None:
                s = s + d_ref[hd, near]
            if mask is not None:
                s = jnp.where(mask, s, NEG_INF)
            m_old = m_s[c]
            m_new = jnp.maximum(m_old, jnp.max(s, axis=1, keepdims=True))
            p = jnp.exp(s - m_new)
            acc_s[c] = jnp.exp(m_old - m_new) * acc_s[c] + _dot(p.astype(BF16), v_s[hd, pl.ds(start, tile), :])
            m_s[c] = m_new

    step(i, 0, visible)

    @pl.when(i >= 1)
    def _():
        step(i - 1, 1, None)

    def body(j, carry):
        step(j, None, None)
        return carry

    lax.fori_loop(0, i - 1, body, 0)
    lam = _diff_lambda(lam_ref[...], lam_init)
    for hd in range(n_heads):
        a1, a2 = acc_s[2 * hd], acc_s[2 * hd + 1]
        o = a1[:, 0:LANES] / a1[:, LANES:2 * LANES] - lam * (a2[:, 0:LANES] / a2[:, LANES:2 * LANES])
        o_ref[:, head(hd)] = _sub_layer_norm(o, g_ref[...], lam_init).astype(BF16)


def _diff_prompt(q12, k12_all, vd_all, bias_tiles, lam_params, subln_g, layer, batch, seq, lam_init):
    tile = min(DIFF_TILE, seq)
    assert tile >= MAX_DISTANCE
    nq = seq // tile
    nh = DIFF_HEADS_PER_STEP
    width = nh * LANES
    body = functools.partial(_diff_prompt_body, tile=tile, lam_init=lam_init, n_heads=nh)
    qmap = lambda b, h, i: (b * nq + i, h)
    kvmap = lambda b, h, i: (layer, b, h)
    return pl.pallas_call(
        body, grid=(batch, DIFF_HEADS // nh, nq),
        in_specs=[pl.BlockSpec((None, 4, 64), lambda b, h, i: (layer, 0, 0)),
                  pl.BlockSpec((None, 1, DIFF_V_DIM), lambda b, h, i: (layer, 0, 0)),
                  pl.BlockSpec((tile, width), qmap),
                  pl.BlockSpec((None, seq, width), kvmap),
                  pl.BlockSpec((None, seq, width), kvmap),
                  pl.BlockSpec((nh, 2, tile, tile), lambda b, h, i: (h, 0, 0, 0))],
        out_specs=pl.BlockSpec((tile, width), qmap),
        out_shape=jax.ShapeDtypeStruct((batch * seq, DIFF_V_WIDTH), BF16),
        scratch_shapes=[pltpu.VMEM((seq, width), BF16), pltpu.VMEM((nh, seq, 2 * LANES), BF16),
                        pltpu.VMEM((2 * nh, tile, 1), F32), pltpu.VMEM((2 * nh, tile, 2 * LANES), F32)],
        compiler_params=_params(("arbitrary", "arbitrary", "arbitrary")),
        name=f"diff_prompt_{layer}")(lam_params, subln_g, q12, k12_all, vd_all, bias_tiles)


def _rows_per_head(x):
    n_rows = 2 * DIFF_HEADS
    head = lax.broadcasted_iota(jnp.int32, (n_rows, LANES), 0) & (DIFF_HEADS - 1)
    out = jnp.zeros((n_rows, LANES), F32)
    for hh in range(DIFF_HEADS):
        out = jnp.where(head == hh, jnp.broadcast_to(x[:, hh * LANES:(hh + 1) * LANES], (n_rows, LANES)), out)
    return out


def _diff_sample_body(pt_ref, lam_ref, g_ref, bias_ref, q_ref, kn_ref, vn_ref, *rest,
                      n_pages_step, n_steps, lam_init):
    del pt_ref
    k_refs = rest[:n_pages_step]
    v_refs = rest[n_pages_step:2 * n_pages_step]
    o_ref, k_s, v_s, m_s, l_s, acc_s = rest[2 * n_pages_step:]
    g = pl.program_id(1)
    n_rows = 2 * DIFF_HEADS
    page_rows = DIFF_HEADS * PAGE_SIZE

    @pl.when(g == 0)
    def _():
        m_s[...] = jnp.full_like(m_s, NEG_INF)
        l_s[...] = jnp.zeros_like(l_s)
        acc_s[...] = jnp.zeros_like(acc_s)

    for p in range(n_pages_step):
        k_s[p * page_rows:(p + 1) * page_rows, :] = k_refs[p][...].astype(BF16)
        v_s[p * page_rows:(p + 1) * page_rows, :] = v_refs[p][...].astype(BF16)

    q8 = _rows_per_head(q_ref[...].astype(F32))
    r8 = lax.broadcasted_iota(jnp.int32, (n_rows, LANES), 0)
    l8 = lax.broadcasted_iota(jnp.int32, (n_rows, LANES), 1)
    q8 = jnp.where((l8 >> 6) == (r8 >> 2), q8, 0.0).astype(BF16)
    shape = (n_rows, n_pages_step * page_rows)
    row = lax.broadcasted_iota(jnp.int32, shape, 0)
    col = lax.broadcasted_iota(jnp.int32, shape, 1)
    own = (col & (DIFF_HEADS - 1)) == (row & (DIFF_HEADS - 1))
    s = _dot_nt(q8, k_s[...])
    last = jnp.where(g == n_steps - 1, bias_ref[0], 0.0)
    split = (n_pages_step - 1) * page_rows
    s = jnp.concatenate([s[:, :split], s[:, split:] + last], axis=1) if split else s + last
    s = jnp.where(own, s, NEG_INF)
    m, l, acc = _softmax_step(s, v_s[...], m_s[...], l_s[...], acc_s[...])
    m_s[...] = m
    l_s[...] = l
    acc_s[...] = acc

    @pl.when(g == n_steps - 1)
    def _():
        k_new = _rows_per_head(kn_ref[...]).astype(BF16).astype(F32)
        v_new = _rows_per_head(vn_ref[...]).astype(BF16).astype(F32)
        s_new = jnp.sum(q8.astype(F32) * k_new, axis=1, keepdims=True) + bias_ref[1][:, 0:1]
        m_new = jnp.maximum(m, s_new)
        a = jnp.exp(m - m_new)
        p_new = jnp.exp(s_new - m_new)
        l_f = a * l + p_new
        acc_f = a * acc + p_new.astype(BF16).astype(F32) * v_new
        d = acc_f / l_f
        lam = _diff_lambda(lam_ref[...], lam_init)
        o = d[0:DIFF_HEADS] - lam * d[DIFF_HEADS:n_rows]
        o_ref[...] = _sub_layer_norm(o, g_ref[...], lam_init).astype(BF16)


def _diff_sample(q12, k_new, v_new, cache_k, cache_v, page_table, bias_s, lam_params, subln_g, layer, lam_init):
    db = q12.shape[0]
    n_pages = page_table.shape[1]
    pps = min(DEC_PAGES, n_pages)
    n_steps = n_pages // pps
    n_rows = 2 * DIFF_HEADS
    width = DIFF_HEADS * PAGE_SIZE

    def page_spec(p):
        return pl.BlockSpec((None, None, width, LANES),
                            lambda b, g, pt: (layer, pt[b, g * pps + p], 0, 0))

    vec = pl.BlockSpec((None, 1, width), lambda b, g, pt: (b, 0, 0))
    grid_spec = pltpu.PrefetchScalarGridSpec(
        num_scalar_prefetch=1, grid=(db, n_steps),
        in_specs=[pl.BlockSpec((None, 4, 64), lambda b, g, pt: (layer, 0, 0)),
                  pl.BlockSpec((None, 1, DIFF_V_DIM), lambda b, g, pt: (layer, 0, 0)),
                  pl.BlockSpec((2, n_rows, width), lambda b, g, pt: (0, 0, 0)),
                  vec, vec, vec] + [page_spec(p) for p in range(pps)] * 2,
        out_specs=pl.BlockSpec((None, DIFF_HEADS, DIFF_V_DIM), lambda b, g, pt: (b, 0, 0)),
        scratch_shapes=[pltpu.VMEM((pps * width, LANES), BF16), pltpu.VMEM((pps * width, LANES), BF16),
                        pltpu.VMEM((n_rows, 1), F32), pltpu.VMEM((n_rows, 1), F32),
                        pltpu.VMEM((n_rows, DIFF_V_DIM), F32)])
    body = functools.partial(_diff_sample_body, n_pages_step=pps, n_steps=n_steps, lam_init=lam_init)
    out = pl.pallas_call(
        body, grid_spec=grid_spec, out_shape=jax.ShapeDtypeStruct((db, DIFF_HEADS, DIFF_V_DIM), BF16),
        compiler_params=_params(("arbitrary", "arbitrary")), name=f"diff_sample_{layer}")(
            page_table, lam_params, subln_g, bias_s, q12.reshape(db, 1, width), k_new.reshape(db, 1, width),
            v_new.reshape(db, 1, width), *([cache_k] * pps), *([cache_v] * pps))
    return out.reshape(db, DIFF_V_WIDTH)


def _merge_body(*refs, alpha, with_router):
    if with_router:
        osb_ref, od_ref, gate_ref, x_ref, wpa_ref, wpb_ref, wo_ref, g_ref, b_ref, wr_ref, o_ref, gates_ref = refs
    else:
        osb_ref, od_ref, gate_ref, x_ref, wpa_ref, wpb_ref, wo_ref, g_ref, b_ref, o_ref = refs
    d = x_ref.shape[1]
    pa = _dot(osb_ref[...], wpa_ref[...])
    pb = _dot(od_ref[...], wpb_ref[...])
    mixed = gate_ref[:, 0:d].astype(F32) * pa + gate_ref[:, d:2 * d].astype(F32) * pb
    mix = _dot(mixed.astype(BF16), wo_ref[...])
    x1 = _layer_norm(alpha * x_ref[...] + mix, g_ref[...], b_ref[...])
    o_ref[...] = x1
    if with_router:
        logits = _dot(x1.astype(BF16), wr_ref[...])
        lane = lax.broadcasted_iota(jnp.int32, logits.shape, 1)
        logits = jnp.where(lane < N_EXPERTS, logits, -jnp.inf)
        lane = lane.astype(F32)
        v1 = jnp.max(logits, axis=1, keepdims=True)
        i1 = jnp.min(jnp.where(logits == v1, lane, float(LANES)), axis=1, keepdims=True)
        rest = jnp.where(lane == i1, -jnp.inf, logits)
        v2 = jnp.max(rest, axis=1, keepdims=True)
        i2 = jnp.min(jnp.where(rest == v2, lane, float(LANES)), axis=1, keepdims=True)
        e = jnp.exp(v2 - v1)
        w1 = 1.0 / (1.0 + e)
        w2 = e / (1.0 + e)
        gates_ref[...] = jnp.where(lane == i1, w1, 0.0) + jnp.where(lane == i2, w2, 0.0)


def _merge(o_sb, o_d, gate, x, w_pa, w_pb, w_o, ln_g, ln_b, layer, alpha, router_w=None, moe_index=None):
    n, d = x.shape
    tm = min(ROW_TILE, n)
    row = lambda i: (i, 0)
    wsel = lambda i: (layer, 0, 0)
    vsel = lambda i: (layer, 0, 0)
    in_specs = [pl.BlockSpec((tm, SB_WIDTH), row), pl.BlockSpec((tm, DIFF_V_WIDTH), row),
                pl.BlockSpec((tm, 2 * d), row), pl.BlockSpec((tm, d), row),
                pl.BlockSpec((None, SB_WIDTH, d), wsel), pl.BlockSpec((None, DIFF_V_WIDTH, d), wsel),
                pl.BlockSpec((None, d, d), wsel), pl.BlockSpec((None, 1, d), vsel), pl.BlockSpec((None, 1, d), vsel)]
    args = [o_sb, o_d, gate, x, w_pa, w_pb, w_o, ln_g, ln_b]
    out_shape = [jax.ShapeDtypeStruct((n, d), F32)]
    out_specs = [pl.BlockSpec((tm, d), row)]
    with_router = router_w is not None
    if with_router:
        in_specs.append(pl.BlockSpec((None, d, LANES), lambda i: (moe_index, 0, 0)))
        args.append(router_w)
        out_shape.append(jax.ShapeDtypeStruct((n, LANES), F32))
        out_specs.append(pl.BlockSpec((tm, LANES), row))
    body = functools.partial(_merge_body, alpha=alpha, with_router=with_router)
    res = pl.pallas_call(
        body, grid=(n // tm,), in_specs=in_specs, out_specs=out_specs, out_shape=out_shape,
        compiler_params=_params(("arbitrary",)), name=f"merge_{layer}_{n}")(*args)
    return (res[0], res[1]) if with_router else (res[0], None)


def _swiglu_hidden(xb, wg, wu):
    hg = _dot(xb, wg)
    hu = _dot(xb, wu)
    return (hg * jax.nn.sigmoid(hg) * hu).astype(BF16)


def _ffn_body(x_ref, wg_ref, wu_ref, wd_ref, g_ref, b_ref, o_ref, acc_s, xb_s, *, alpha, n_ff):
    j = pl.program_id(1)

    @pl.when(j == 0)
    def _():
        acc_s[...] = jnp.zeros_like(acc_s)
        xb_s[...] = x_ref[...].astype(BF16)

    h = _swiglu_hidden(xb_s[...], wg_ref[...], wu_ref[...])
    acc_s[...] += _dot(h, wd_ref[...])

    @pl.when(j == n_ff - 1)
    def _():
        o_ref[...] = _layer_norm(alpha * x_ref[...] + acc_s[...], g_ref[...], b_ref[...])


def _ffn_dense(x, wg, wu, wd, ln_g, ln_b, layer, index, alpha):
    n, d = x.shape
    f = wg.shape[2]
    tm = min(FFN_ROW_TILE, n)
    tf = _largest_tile(f, 256)
    n_ff = f // tf
    body = functools.partial(_ffn_body, alpha=alpha, n_ff=n_ff)
    return pl.pallas_call(
        body, grid=(n // tm, n_ff),
        in_specs=[pl.BlockSpec((tm, d), lambda i, j: (i, 0)),
                  pl.BlockSpec((None, d, tf), lambda i, j: (index, 0, j)),
                  pl.BlockSpec((None, d, tf), lambda i, j: (index, 0, j)),
                  pl.BlockSpec((None, tf, d), lambda i, j: (index, j, 0)),
                  pl.BlockSpec((None, 1, d), lambda i, j: (layer, 0, 0)),
                  pl.BlockSpec((None, 1, d), lambda i, j: (layer, 0, 0))],
        out_specs=pl.BlockSpec((tm, d), lambda i, j: (i, 0)),
        out_shape=jax.ShapeDtypeStruct((n, d), F32),
        scratch_shapes=[pltpu.VMEM((tm, d), F32), pltpu.VMEM((tm, d), BF16)],
        compiler_params=_params(("arbitrary", "arbitrary")), name=f"ffn_{layer}_{n}")(x, wg, wu, wd, ln_g, ln_b)


def _moe_body(x_ref, gates_ref, wg_ref, wu_ref, wd_ref, g_ref, b_ref, o_ref, acc_s, xb_s, *, alpha, n_ff):
    e = pl.program_id(1)
    j = pl.program_id(2)

    @pl.when(jnp.logical_and(e == 0, j == 0))
    def _():
        acc_s[...] = jnp.zeros_like(acc_s)
        xb_s[...] = x_ref[...].astype(BF16)

    h = _swiglu_hidden(xb_s[...], wg_ref[...].astype(BF16), wu_ref[...].astype(BF16))
    y = _dot(h, wd_ref[...].astype(BF16))
    gates = gates_ref[...]
    lane = lax.broadcasted_iota(jnp.int32, gates.shape, 1)
    gate_e = jnp.sum(jnp.where(lane == e, gates, 0.0), axis=1, keepdims=True)
    acc_s[...] += gate_e * y

    @pl.when(jnp.logical_and(e == N_EXPERTS - 1, j == n_ff - 1))
    def _():
        o_ref[...] = _layer_norm(alpha * x_ref[...] + acc_s[...], g_ref[...], b_ref[...])


def _ffn_moe(x, gates, wg, wu, wd, ln_g, ln_b, layer, index, alpha):
    n, d = x.shape
    f = wg.shape[3]
    tm = min(FFN_ROW_TILE, n)
    tf = _largest_tile(f, 256)
    n_ff = f // tf
    body = functools.partial(_moe_body, alpha=alpha, n_ff=n_ff)
    return pl.pallas_call(
        body, grid=(n // tm, N_EXPERTS, n_ff),
        in_specs=[pl.BlockSpec((tm, d), lambda i, e, j: (i, 0)),
                  pl.BlockSpec((tm, LANES), lambda i, e, j: (i, 0)),
                  pl.BlockSpec((None, None, d, tf), lambda i, e, j: (index, e, 0, j)),
                  pl.BlockSpec((None, None, d, tf), lambda i, e, j: (index, e, 0, j)),
                  pl.BlockSpec((None, None, tf, d), lambda i, e, j: (index, e, j, 0)),
                  pl.BlockSpec((None, 1, d), lambda i, e, j: (layer, 0, 0)),
                  pl.BlockSpec((None, 1, d), lambda i, e, j: (layer, 0, 0))],
        out_specs=pl.BlockSpec((tm, d), lambda i, e, j: (i, 0)),
        out_shape=jax.ShapeDtypeStruct((n, d), F32),
        scratch_shapes=[pltpu.VMEM((tm, d), F32), pltpu.VMEM((tm, d), BF16)],
        compiler_params=_params(("arbitrary", "arbitrary", "arbitrary")),
        name=f"moe_{layer}_{n}")(x, gates, wg, wu, wd, ln_g, ln_b)


def _prepare_in_proj(w_in):
    depth, d, _ = w_in.shape

    def pair(a, b):
        a = a.reshape(depth, d, DIFF_HEADS, 64)
        b = b.reshape(depth, d, DIFF_HEADS, 64)
        return jnp.concatenate([a, b], axis=-1).reshape(depth, d, 2 * DIFF_QK_WIDTH)

    c = [0, 512, 1024, 1536, 1792, 2048, 2304, 2560, 3072, 3072 + 2 * d]
    part = [w_in[:, :, c[k]:c[k + 1]] for k in range(9)]
    q_sb, k_sb, v_sb, q1, q2, k1, k2, v_d, gate = part
    w_main = jnp.concatenate([q_sb, pair(q1, q2), pair(k1, k2), v_d, gate], axis=-1).astype(BF16)
    w_kvt = jnp.swapaxes(jnp.concatenate([k_sb, v_sb], axis=-1), 1, 2).astype(BF16)
    return w_main, w_kvt


def kernel(x_prompt, x_sample, cache_sb_k, cache_sb_v, cache_diff_k, cache_diff_v, page_table, rel_bias, w_in,
           w_pa, w_pb, w_o, diff_lambda, diff_subln_g, ln1_g, ln1_b, ln2_g, ln2_b, ffn_w_gate, ffn_w_up,
           ffn_w_down, router_w, moe_w_gate, moe_w_up, moe_w_down):
    batch, seq, d = x_prompt.shape
    db = x_sample.shape[0]
    depth = w_in.shape[0]
    n_pool = cache_sb_k.shape[1]
    alpha = (2.0 * depth) ** 0.25
    n = batch * seq

    w_main, w_kvt = _prepare_in_proj(w_in)
    w_pa_b, w_pb_b, w_o_b = w_pa.astype(BF16), w_pb.astype(BF16), w_o.astype(BF16)
    ffn_g_b, ffn_u_b, ffn_d_b = ffn_w_gate.astype(BF16), ffn_w_up.astype(BF16), ffn_w_down.astype(BF16)
    router_b = jnp.pad(router_w, ((0, 0), (0, 0), (0, LANES - N_EXPERTS))).astype(BF16)

    sbk_t = jnp.transpose(cache_sb_k, (0, 1, 3, 4, 2)).reshape(depth, n_pool, SB_WIDTH, PAGE_SIZE)
    sbv_t = jnp.transpose(cache_sb_v, (0, 1, 3, 4, 2)).reshape(depth, n_pool, SB_WIDTH, PAGE_SIZE)
    dk_r = cache_diff_k.reshape(depth, n_pool, PAGE_SIZE * DIFF_HEADS, LANES)
    dv_r = cache_diff_v.reshape(depth, n_pool, PAGE_SIZE * DIFF_HEADS, LANES)

    ln1_g, ln1_b, ln2_g, ln2_b = (v.reshape(depth, 1, d) for v in (ln1_g, ln1_b, ln2_g, ln2_b))
    diff_subln_g = diff_subln_g.reshape(depth, 1, DIFF_V_DIM)

    bias_t = _bias_tiles(rel_bias, min(DIFF_TILE, seq))
    bias_s = _bias_sample(rel_bias)

    xp = x_prompt.reshape(n, d)
    xs = x_sample.reshape(db, d)
    stacked = None
    s_ksb, s_vsb, s_k12, s_vd = [], [], [], []
    for l in range(depth):
        lam_init = 0.8 - 0.6 * math.exp(-0.3 * l)
        is_moe = l % 2 == 1
        idx = l // 2

        q_sb, q12, gate, k12_all, vd_all, kt_all, vt_all = _proj_prompt(xp, w_main, w_kvt, l, stacked, batch, seq)
        stacked = (k12_all, vd_all, kt_all, vt_all)
        o_sb = _sb_prompt(q_sb, kt_all, vt_all, l, batch, seq)
        o_d = _diff_prompt(q12, k12_all, vd_all, bias_t, diff_lambda, diff_subln_g, l, batch, seq, lam_init)
        xp, gates = _merge(o_sb, o_d, gate, xp, w_pa_b, w_pb_b, w_o_b, ln1_g, ln1_b, l, alpha,
                           router_b if is_moe else None, idx)
        if is_moe:
            xp = _ffn_moe(xp, gates, moe_w_gate, moe_w_up, moe_w_down, ln2_g, ln2_b, l, idx, alpha)
        else:
            xp = _ffn_dense(xp, ffn_g_b, ffn_u_b, ffn_d_b, ln2_g, ln2_b, l, idx, alpha)

        sq_sb, sq12, sgate, sk12, svd, sksb, svsb = _proj_sample(xs, w_main, w_kvt, l)
        so_sb = _sb_sample(sq_sb, sbk_t, sbv_t, page_table, l)
        so_d = _diff_sample(sq12, sk12, svd, dk_r, dv_r, page_table, bias_s, diff_lambda, diff_subln_g, l, lam_init)
        xs, sgates = _merge(so_sb, so_d, sgate, xs, w_pa_b, w_pb_b, w_o_b, ln1_g, ln1_b, l, alpha,
                            router_b if is_moe else None, idx)
        if is_moe:
            xs = _ffn_moe(xs, sgates, moe_w_gate, moe_w_up, moe_w_down, ln2_g, ln2_b, l, idx, alpha)
        else:
            xs = _ffn_dense(xs, ffn_g_b, ffn_u_b, ffn_d_b, ln2_g, ln2_b, l, idx, alpha)
        s_ksb.append(sksb)
        s_vsb.append(svsb)
        s_k12.append(sk12)
        s_vd.append(svd)

    k12_all, vd_all, kt_all, vt_all = stacked
    to_sb = lambda t: jnp.transpose(t.reshape(depth, batch, SB_HEADS, 64, seq), (0, 1, 4, 2, 3))
    to_diff = lambda t: t.reshape(depth, batch, seq, DIFF_HEADS, DIFF_V_DIM)
    return (xp.reshape(batch, seq, d), xs.reshape(db, 1, d),
            to_sb(kt_all), to_sb(vt_all), to_diff(k12_all), to_diff(vd_all),
            jnp.stack(s_ksb).reshape(depth, db, 1, SB_HEADS, 64),
            jnp.stack(s_vsb).reshape(depth, db, 1, SB_HEADS, 64),
            jnp.stack(s_k12).reshape(depth, db, 1, DIFF_HEADS, DIFF_V_DIM),
            jnp.stack(s_vd).reshape(depth, db, 1, DIFF_HEADS, DIFF_V_DIM))
```

```python
import functools
import math

import jax
import jax.numpy as jnp
from jax import lax
from jax.experimental import pallas as pl
from jax.experimental.pallas import tpu as pltpu

F32 = jnp.float32
BF16 = jnp.bfloat16

PAGE_SIZE = 128
SB_HEADS = 8
SB_WIDTH = 512
DIFF_HEADS = 4
DIFF_V_DIM = 128
DIFF_QK_WIDTH = 256
DIFF_V_WIDTH = 512
N_EXPERTS = 8
TOP_K = 2
N_BUCKETS = 32
MAX_DISTANCE = 128
LN_EPS = 1e-5
SUBLN_EPS = 1e-5
NEG_INF = -1e30
QK_SCALE = 0.125

LANES = 128
SUBLANES = 8
VMEM_LIMIT_BYTES = 56 * 1024 * 1024

ROW_TILE = 512
FFN_ROW_TILE = 1024
DIFF_TILE = 512
SB_TILE_Q = 512
SB_TILE_K = 512
DEC_PAGES = 8
SB_HEADS_PER_STEP = 4
DIFF_HEADS_PER_STEP = 2
MOE_ROW_TILE = 1024
MOE_TOKEN_TILE = 256

NT_DIMS = (((1,), (1,)), ((), ()))


def _params(semantics):
    return pltpu.CompilerParams(dimension_semantics=semantics, vmem_limit_bytes=VMEM_LIMIT_BYTES)


def _dot(a, b):
    return jnp.dot(a, b, preferred_element_type=F32)


def _dot_nt(a, b):
    return lax.dot_general(a, b, NT_DIMS, preferred_element_type=F32)


def _layer_norm(y, g, b):
    mu = jnp.mean(y, axis=-1, keepdims=True)
    d = y - mu
    var = jnp.mean(d * d, axis=-1, keepdims=True)
    return d * lax.rsqrt(var + LN_EPS) * g + b


def _strict_lower_ones(n):
    j = lax.broadcasted_iota(jnp.int32, (n, n), 0)
    s = lax.broadcasted_iota(jnp.int32, (n, n), 1)
    return jnp.where(j > s, 1.0, 0.0).astype(BF16)


def _largest_tile(dim, cap):
    if dim <= cap:
        return dim
    best = None
    for t in range(LANES, cap + 1, LANES):
        if dim % t == 0:
            best = t
    assert best is not None, (dim, cap)
    return best


def _proj_prompt_body(x_ref, wm_ref, wkvt_ref, *rest):
    qsb_ref, q12_ref, gate_ref, k12_ref, vd_ref, ksbt_ref, vsbt_ref = rest[-7:]
    x = x_ref[...].astype(BF16)

    def cols(c0, n):
        return _dot(x, wm_ref[:, c0:c0 + n])

    qsb_ref[...] = (cols(0, 512) * QK_SCALE).astype(BF16)
    q12_ref[...] = (cols(512, 512) * QK_SCALE).astype(BF16)
    k12_ref[...] = cols(1024, 512)
    vd_ref[...] = cols(1536, 512)
    gate_ref[...] = jax.nn.sigmoid(cols(2048, 2048)).astype(BF16)
    ksbt_ref[...] = _dot_nt(wkvt_ref[0:512, :], x)
    vsbt_ref[...] = _dot_nt(wkvt_ref[512:1024, :], x)


def _proj_prompt(x, w_main, w_kvt, layer, stacked, batch, seq):
    n, d = x.shape
    depth = w_main.shape[0]
    tm = min(ROW_TILE, seq)
    nt = seq // tm
    grid = (n // tm,)
    row = lambda i: (i, 0)
    in_specs = [
        pl.BlockSpec((tm, d), row),
        pl.BlockSpec((None, d, w_main.shape[2]), lambda i: (layer, 0, 0)),
        pl.BlockSpec((None, w_kvt.shape[1], d), lambda i: (layer, 0, 0)),
    ]
    args = [x, w_main, w_kvt]
    aliases = {}
    if stacked is not None:
        in_specs += [pl.BlockSpec(memory_space=pl.ANY)] * 4
        args += list(stacked)
        aliases = {3: 3, 4: 4, 5: 5, 6: 6}
    out_shape = [
        jax.ShapeDtypeStruct((n, 512), BF16),
        jax.ShapeDtypeStruct((n, 512), BF16),
        jax.ShapeDtypeStruct((n, 2048), BF16),
        jax.ShapeDtypeStruct((depth, n, 512), F32),
        jax.ShapeDtypeStruct((depth, n, 512), F32),
        jax.ShapeDtypeStruct((depth, batch, 512, seq), F32),
        jax.ShapeDtypeStruct((depth, batch, 512, seq), F32),
    ]
    out_specs = [
        pl.BlockSpec((tm, 512), row),
        pl.BlockSpec((tm, 512), row),
        pl.BlockSpec((tm, 2048), row),
        pl.BlockSpec((None, tm, 512), lambda i: (layer, i, 0)),
        pl.BlockSpec((None, tm, 512), lambda i: (layer, i, 0)),
        pl.BlockSpec((None, None, 512, tm), lambda i: (layer, i // nt, 0, i % nt)),
        pl.BlockSpec((None, None, 512, tm), lambda i: (layer, i // nt, 0, i % nt)),
    ]
    return pl.pallas_call(
        _proj_prompt_body, grid=grid, in_specs=in_specs, out_specs=out_specs, out_shape=out_shape,
        input_output_aliases=aliases, compiler_params=_params(("arbitrary",)),
        name=f"proj_prompt_{layer}")(*args)


def _proj_sample_body(x_ref, wm_ref, wkvt_ref, qsb_ref, q12_ref, gate_ref, k12_ref, vd_ref, ksb_ref, vsb_ref):
    x = x_ref[...].astype(BF16)

    def cols(c0, n):
        return _dot(x, wm_ref[:, c0:c0 + n])

    qsb_ref[...] = (cols(0, 512) * QK_SCALE).astype(BF16)
    q12_ref[...] = (cols(512, 512) * QK_SCALE).astype(BF16)
    k12_ref[...] = cols(1024, 512)
    vd_ref[...] = cols(1536, 512)
    gate_ref[...] = jax.nn.sigmoid(cols(2048, 2048)).astype(BF16)
    ksb_ref[...] = _dot_nt(x, wkvt_ref[0:512, :])
    vsb_ref[...] = _dot_nt(x, wkvt_ref[512:1024, :])


def _proj_sample(x, w_main, w_kvt, layer):
    n, d = x.shape
    full = lambda shape: pl.BlockSpec(shape, lambda i: (0,) * len(shape))
    out_shape = [
        jax.ShapeDtypeStruct((n, 512), BF16), jax.ShapeDtypeStruct((n, 512), BF16),
        jax.ShapeDtypeStruct((n, 2048), BF16),
        jax.ShapeDtypeStruct((n, 512), F32), jax.ShapeDtypeStruct((n, 512), F32),
        jax.ShapeDtypeStruct((n, 512), F32), jax.ShapeDtypeStruct((n, 512), F32),
    ]
    return pl.pallas_call(
        _proj_sample_body, grid=(1,),
        in_specs=[full((n, d)),
                  pl.BlockSpec((None, d, w_main.shape[2]), lambda i: (layer, 0, 0)),
                  pl.BlockSpec((None, w_kvt.shape[1], d), lambda i: (layer, 0, 0))],
        out_specs=[full(s.shape) for s in out_shape], out_shape=out_shape,
        compiler_params=_params(("arbitrary",)), name=f"proj_sample_{layer}")(x, w_main, w_kvt)


def _softplus(z):
    return jnp.maximum(z, 0.0) + jnp.log(1.0 + jnp.exp(-jnp.abs(z)))


def _neg_suffix_sums(sp, neg_tri):
    hi = sp.astype(BF16)
    lo = (sp - hi.astype(F32)).astype(BF16)
    return _dot(hi, neg_tri) + _dot(lo, neg_tri)


def _sb_block(q, k_t, v_t, neg_tri, r, acc, causal):
    z = _dot(q, k_t)
    sp = _softplus(z)
    log_beta = z - sp
    if causal is not None:
        sp = jnp.where(causal, sp, 0.0)
    cum = _neg_suffix_sums(sp, neg_tri)
    w = jnp.exp(log_beta + cum + r)
    if causal is not None:
        w = jnp.where(causal, w, 0.0)
    acc = acc + _dot_nt(w.astype(BF16), v_t)
    r = r + cum[:, 0:1] - sp[:, 0:1]
    return r, acc


def _sb_prompt_body(q_ref, kt_ref, vt_ref, o_ref, kt_s, vt_s, r_s, acc_s, *, tq, tk, n_blocks, n_pairs):
    i = pl.program_id(2)
    ratio = tq // tk

    @pl.when(i == 0)
    def _():
        for c in range(n_blocks):
            kt_s[c] = kt_ref[:, c * tk:(c + 1) * tk].astype(BF16)
            vt_s[c] = vt_ref[:, c * tk:(c + 1) * tk].astype(BF16)

    lane = lax.broadcasted_iota(jnp.int32, (tq, LANES), 1)
    q_heads = []
    for hp in range(n_pairs):
        q2 = q_ref[:, hp * LANES:(hp + 1) * LANES]
        zero = jnp.zeros_like(q2)
        q_heads += [jnp.where(lane < 64, q2, zero), jnp.where(lane >= 64, q2, zero)]
    row = lax.broadcasted_iota(jnp.int32, (tq, tk), 0)
    col = lax.broadcasted_iota(jnp.int32, (tq, tk), 1)
    neg_tri = -_strict_lower_ones(tk)
    r_s[...] = jnp.zeros_like(r_s)
    acc_s[...] = jnp.zeros_like(acc_s)

    def step(kb, mask):
        for h in range(2 * n_pairs):
            rows = slice((h // 2) * LANES, (h // 2 + 1) * LANES)
            r, acc = _sb_block(q_heads[h], kt_s[kb, rows, :], vt_s[kb, rows, :], neg_tri, r_s[h], acc_s[h], mask)
            r_s[h] = r
            acc_s[h] = acc

    for d in reversed(range(ratio)):
        step(i * ratio + d, col + d * tk < row)

    def body(j, carry):
        step(i * ratio - 1 - j, None)
        return carry

    lax.fori_loop(0, i * ratio, body, 0)
    for hp in range(n_pairs):
        o_ref[:, hp * LANES:(hp + 1) * LANES] = jnp.where(lane < 64, acc_s[2 * hp], acc_s[2 * hp + 1]).astype(BF16)


def _sb_prompt(q, kt_all, vt_all, layer, batch, seq):
    tq = min(SB_TILE_Q, seq)
    tk = min(SB_TILE_K, tq)
    nq = seq // tq
    n_pairs = SB_HEADS_PER_STEP // 2
    width = n_pairs * LANES
    body = functools.partial(_sb_prompt_body, tq=tq, tk=tk, n_blocks=seq // tk, n_pairs=n_pairs)
    qmap = lambda b, h, i: (b * nq + i, h)
    kvmap = lambda b, h, i: (layer, b, h, 0)
    return pl.pallas_call(
        body, grid=(batch, SB_HEADS // SB_HEADS_PER_STEP, nq),
        in_specs=[pl.BlockSpec((tq, width), qmap),
                  pl.BlockSpec((None, None, width, seq), kvmap),
                  pl.BlockSpec((None, None, width, seq), kvmap)],
        out_specs=pl.BlockSpec((tq, width), qmap),
        out_shape=jax.ShapeDtypeStruct((batch * seq, SB_WIDTH), BF16),
        scratch_shapes=[pltpu.VMEM((seq // tk, width, tk), BF16), pltpu.VMEM((seq // tk, width, tk), BF16),
                        pltpu.VMEM((SB_HEADS_PER_STEP, tq, 1), F32),
                        pltpu.VMEM((SB_HEADS_PER_STEP, tq, LANES), F32)],
        compiler_params=_params(("arbitrary", "arbitrary", "arbitrary")),
        name=f"sb_prompt_{layer}")(q, kt_all, vt_all)


def _sb_sample_body(pt_ref, q_ref, *rest, n_pages_step, n_steps):
    del pt_ref
    k_refs = rest[:n_pages_step]
    v_refs = rest[n_pages_step:2 * n_pages_step]
    o_ref, kt_s, vt_s, r_s, acc_s = rest[2 * n_pages_step:]
    g = pl.program_id(1)
    page = lambda p: slice(p * PAGE_SIZE, (p + 1) * PAGE_SIZE)

    @pl.when(g == 0)
    def _():
        r_s[...] = jnp.zeros_like(r_s)
        acc_s[...] = jnp.zeros_like(acc_s)

    for p in range(n_pages_step):
        kt_s[:, page(p)] = k_refs[p][...].astype(BF16)
        vt_s[:, page(p)] = v_refs[p][...].astype(BF16)

    q = q_ref[...]
    row = lax.broadcasted_iota(jnp.int32, (SB_HEADS, SB_WIDTH), 0)
    col = lax.broadcasted_iota(jnp.int32, (SB_HEADS, SB_WIDTH), 1)
    own = (col >> 6) == row
    q_bd = jnp.where(own, jnp.broadcast_to(q.astype(F32), (SB_HEADS, SB_WIDTH)), 0.0).astype(BF16)
    z = _dot(q_bd, kt_s[...])
    sp = _softplus(z)
    log_beta = z - sp
    sp_rows = jnp.concatenate([sp[:, page(p)] for p in range(n_pages_step)], axis=0)
    cum_rows = _neg_suffix_sums(sp_rows, -_strict_lower_ones(PAGE_SIZE))
    r = r_s[...]
    w = [None] * n_pages_step
    for p in reversed(range(n_pages_step)):
        cum = cum_rows[p * SB_HEADS:(p + 1) * SB_HEADS]
        sp_p = sp[:, page(p)]
        w[p] = jnp.exp(log_beta[:, page(p)] + cum + r)
        r = r + jnp.broadcast_to(cum[:, 0:1] - sp_p[:, 0:1], r.shape)
    r_s[...] = r
    acc = acc_s[...] + _dot_nt(jnp.concatenate(w, axis=1).astype(BF16), vt_s[...])
    acc_s[...] = acc

    @pl.when(g == n_steps - 1)
    def _():
        o_ref[...] = jnp.sum(jnp.where(own, acc, 0.0), axis=0, keepdims=True).astype(BF16)


def _sb_sample(q, cache_kt, cache_vt, page_table, layer):
    db = q.shape[0]
    n_pages = page_table.shape[1]
    pps = min(DEC_PAGES, n_pages)
    n_steps = n_pages // pps
    q3 = q.reshape(db, 1, SB_WIDTH)

    def page_spec(p):
        return pl.BlockSpec((None, None, SB_WIDTH, PAGE_SIZE),
                            lambda b, g, pt: (layer, pt[b, (n_steps - 1 - g) * pps + p], 0, 0))

    qspec = pl.BlockSpec((None, 1, SB_WIDTH), lambda b, g, pt: (b, 0, 0))
    grid_spec = pltpu.PrefetchScalarGridSpec(
        num_scalar_prefetch=1, grid=(db, n_steps),
        in_specs=[qspec] + [page_spec(p) for p in range(pps)] * 2,
        out_specs=qspec,
        scratch_shapes=[pltpu.VMEM((SB_WIDTH, pps * PAGE_SIZE), BF16), pltpu.VMEM((SB_WIDTH, pps * PAGE_SIZE), BF16),
                        pltpu.VMEM((SB_HEADS, LANES), F32), pltpu.VMEM((SB_HEADS, SB_WIDTH), F32)])
    body = functools.partial(_sb_sample_body, n_pages_step=pps, n_steps=n_steps)
    out = pl.pallas_call(
        body, grid_spec=grid_spec, out_shape=jax.ShapeDtypeStruct((db, 1, SB_WIDTH), BF16),
        compiler_params=_params(("arbitrary", "arbitrary")), name=f"sb_sample_{layer}")(
            page_table, q3, *([cache_kt] * pps), *([cache_vt] * pps))
    return out.reshape(db, SB_WIDTH)


def _t5_bucket(n):
    max_exact = N_BUCKETS // 2
    nf = jnp.maximum(n, 1).astype(F32)
    large = max_exact + (jnp.log(nf / max_exact) / math.log(MAX_DISTANCE / max_exact)
                         * (N_BUCKETS - max_exact)).astype(jnp.int32)
    large = jnp.minimum(large, N_BUCKETS - 1)
    return jnp.where(n < max_exact, n, large)


def _bias_lookup(bucket, head, rb_ref, n_heads_static):
    val = jnp.zeros(bucket.shape, F32)
    for k in range(N_BUCKETS):
        if isinstance(head, int):
            val = jnp.where(bucket == k, rb_ref[k, head], val)
        else:
            for hh in range(n_heads_static):
                val = jnp.where((bucket == k) & (head == hh), rb_ref[k, hh], val)
    return val


def _bias_tiles_body(rb_ref, o_ref, *, tile):
    h = pl.program_id(0)
    which = pl.program_id(1)
    row = lax.broadcasted_iota(jnp.int32, (tile, tile), 0)
    col = lax.broadcasted_iota(jnp.int32, (tile, tile), 1)
    n = jnp.maximum(row - col + which * tile, 0)
    bucket = _t5_bucket(n)
    val = jnp.zeros((tile, tile), F32)
    for k in range(N_BUCKETS):
        val = jnp.where(bucket == k, rb_ref[k, h], val)
    o_ref[...] = val - rb_ref[N_BUCKETS - 1, h]


def _bias_tiles(rel_bias, tile):
    return pl.pallas_call(
        functools.partial(_bias_tiles_body, tile=tile), grid=(DIFF_HEADS, 2),
        in_specs=[pl.BlockSpec(memory_space=pltpu.SMEM)],
        out_specs=pl.BlockSpec((None, None, tile, tile), lambda h, w: (h, w, 0, 0)),
        out_shape=jax.ShapeDtypeStruct((DIFF_HEADS, 2, tile, tile), F32),
        compiler_params=_params(("arbitrary", "arbitrary")), name="bias_tiles")(rel_bias)


def _bias_sample_body(rb_ref, o_ref):
    shape = (2 * DIFF_HEADS, DIFF_HEADS * PAGE_SIZE)
    head = lax.broadcasted_iota(jnp.int32, shape, 0) & (DIFF_HEADS - 1)
    pos = lax.broadcasted_iota(jnp.int32, shape, 1) >> 2
    far = jnp.zeros(shape, jnp.int32) + MAX_DISTANCE
    last = PAGE_SIZE - pos
    new = jnp.zeros(shape, jnp.int32)
    far_bias = _bias_lookup(_t5_bucket(far), head, rb_ref, DIFF_HEADS)
    for idx, n in enumerate((last, new)):
        o_ref[idx] = _bias_lookup(_t5_bucket(n), head, rb_ref, DIFF_HEADS) - far_bias


def _bias_sample(rel_bias):
    shape = (2, 2 * DIFF_HEADS, DIFF_HEADS * PAGE_SIZE)
    return pl.pallas_call(
        _bias_sample_body, grid=(1,),
        in_specs=[pl.BlockSpec(memory_space=pltpu.SMEM)],
        out_specs=pl.BlockSpec(shape, lambda i: (0, 0, 0)),
        out_shape=jax.ShapeDtypeStruct(shape, F32),
        compiler_params=_params(("arbitrary",)), name="bias_sample")(rel_bias)


def _diff_lambda(lp, lam_init):
    a = jnp.sum(lp[0:1] * lp[1:2], axis=1, keepdims=True)
    b = jnp.sum(lp[2:3] * lp[3:4], axis=1, keepdims=True)
    return jnp.exp(a) - jnp.exp(b) + lam_init


def _softmax_step(s, v, m, l, acc):
    m_new = jnp.maximum(m, jnp.max(s, axis=1, keepdims=True))
    a = jnp.exp(m - m_new)
    p = jnp.exp(s - m_new)
    l = a * l + jnp.sum(p, axis=1, keepdims=True)
    acc = a * acc + _dot(p.astype(BF16), v)
    return m_new, l, acc


def _sub_layer_norm(o, g, lam_init):
    o = o * lax.rsqrt(jnp.mean(o * o, axis=-1, keepdims=True) + SUBLN_EPS)
    return o * g * (1.0 - lam_init)


def _diff_prompt_body(lam_ref, g_ref, q_ref, k_ref, v_ref, d_ref, o_ref,
                      k_s, v_s, m_s, acc_s, *, tile, lam_init, n_heads):
    i = pl.program_id(2)
    head = lambda hd: slice(hd * LANES, (hd + 1) * LANES)

    @pl.when(i == 0)
    def _():
        k_s[...] = k_ref[...].astype(BF16)
        for hd in range(n_heads):
            v_s[hd, :, 0:LANES] = v_ref[:, head(hd)].astype(BF16)
            v_s[hd, :, LANES:2 * LANES] = jnp.ones((v_s.shape[1], LANES), BF16)

    lane = lax.broadcasted_iota(jnp.int32, (tile, LANES), 1)
    q_maps = []
    for hd in range(n_heads):
        q2 = q_ref[:, head(hd)]
        zero = jnp.zeros_like(q2)
        q_maps += [jnp.where(lane < 64, q2, zero), jnp.where(lane >= 64, q2, zero)]
    row = lax.broadcasted_iota(jnp.int32, (tile, tile), 0)
    col = lax.broadcasted_iota(jnp.int32, (tile, tile), 1)
    visible = col <= row
    m_s[...] = jnp.full_like(m_s, NEG_INF)
    acc_s[...] = jnp.zeros_like(acc_s)

    def step(kb, near, mask):
        start = pl.multiple_of(kb * tile, tile)
        for c in range(2 * n_heads):
            hd = c // 2
            s = _dot_nt(q_maps[c], k_s[pl.ds(start, tile), head(hd)])
            if near is not None:
                s = s + d_ref[hd, near]
            if mask is not None:
                s = jnp.where(mask, s, NEG_INF)
            m_old = m_s[c]
            m_new = jnp.maximum(m_old, jnp.max(s, axis=1, keepdims=True))
            p = jnp.exp(s - m_new)
            acc_s[c] = jnp.exp(m_old - m_new) * acc_s[c] + _dot(p.astype(BF16), v_s[hd, pl.ds(start, tile), :])
            m_s[c] = m_new

    step(i, 0, visible)

    @pl.when(i >= 1)
    def _():
        step(i - 1, 1, None)

    def body(j, carry):
        step(j, None, None)
        return carry

    lax.fori_loop(0, i - 1, body, 0)
    lam = _diff_lambda(lam_ref[...], lam_init)
    for hd in range(n_heads):
        a1, a2 = acc_s[2 * hd], acc_s[2 * hd + 1]
        o = a1[:, 0:LANES] / a1[:, LANES:2 * LANES] - lam * (a2[:, 0:LANES] / a2[:, LANES:2 * LANES])
        o_ref[:, head(hd)] = _sub_layer_norm(o, g_ref[...], lam_init).astype(BF16)


def _diff_prompt(q12, k12_all, vd_all, bias_tiles, lam_params, subln_g, layer, batch, seq, lam_init):
    tile = min(DIFF_TILE, seq)
    assert tile >= MAX_DISTANCE
    nq = seq // tile
    nh = DIFF_HEADS_PER_STEP
    width = nh * LANES
    body = functools.partial(_diff_prompt_body, tile=tile, lam_init=lam_init, n_heads=nh)
    qmap = lambda b, h, i: (b * nq + i, h)
    kvmap = lambda b, h, i: (layer, b, h)
    return pl.pallas_call(
        body, grid=(batch, DIFF_HEADS // nh, nq),
        in_specs=[pl.BlockSpec((None, 4, 64), lambda b, h, i: (layer, 0, 0)),
                  pl.BlockSpec((None, 1, DIFF_V_DIM), lambda b, h, i: (layer, 0, 0)),
                  pl.BlockSpec((tile, width), qmap),
                  pl.BlockSpec((None, seq, width), kvmap),
                  pl.BlockSpec((None, seq, width), kvmap),
                  pl.BlockSpec((nh, 2, tile, tile), lambda b, h, i: (h, 0, 0, 0))],
        out_specs=pl.BlockSpec((tile, width), qmap),
        out_shape=jax.ShapeDtypeStruct((batch * seq, DIFF_V_WIDTH), BF16),
        scratch_shapes=[pltpu.VMEM((seq, width), BF16), pltpu.VMEM((nh, seq, 2 * LANES), BF16),
                        pltpu.VMEM((2 * nh, tile, 1), F32), pltpu.VMEM((2 * nh, tile, 2 * LANES), F32)],
        compiler_params=_params(("arbitrary", "arbitrary", "arbitrary")),
        name=f"diff_prompt_{layer}")(lam_params, subln_g, q12, k12_all, vd_all, bias_tiles)


def _rows_per_head(x):
    n_rows = 2 * DIFF_HEADS
    head = lax.broadcasted_iota(jnp.int32, (n_rows, LANES), 0) & (DIFF_HEADS - 1)
    out = jnp.zeros((n_rows, LANES), F32)
    for hh in range(DIFF_HEADS):
        out = jnp.where(head == hh, jnp.broadcast_to(x[:, hh * LANES:(hh + 1) * LANES], (n_rows, LANES)), out)
    return out


def _diff_sample_body(pt_ref, lam_ref, g_ref, bias_ref, q_ref, kn_ref, vn_ref, *rest,
                      n_pages_step, n_steps, lam_init):
    del pt_ref
    k_refs = rest[:n_pages_step]
    v_refs = rest[n_pages_step:2 * n_pages_step]
    o_ref, k_s, v_s, m_s, l_s, acc_s = rest[2 * n_pages_step:]
    g = pl.program_id(1)
    n_rows = 2 * DIFF_HEADS
    page_rows = DIFF_HEADS * PAGE_SIZE

    @pl.when(g == 0)
    def _():
        m_s[...] = jnp.full_like(m_s, NEG_INF)
        l_s[...] = jnp.zeros_like(l_s)
        acc_s[...] = jnp.zeros_like(acc_s)

    for p in range(n_pages_step):
        k_s[p * page_rows:(p + 1) * page_rows, :] = k_refs[p][...].astype(BF16)
        v_s[p * page_rows:(p + 1) * page_rows, :] = v_refs[p][...].astype(BF16)

    q8 = _rows_per_head(q_ref[...].astype(F32))
    r8 = lax.broadcasted_iota(jnp.int32, (n_rows, LANES), 0)
    l8 = lax.broadcasted_iota(jnp.int32, (n_rows, LANES), 1)
    q8 = jnp.where((l8 >> 6) == (r8 >> 2), q8, 0.0).astype(BF16)
    shape = (n_rows, n_pages_step * page_rows)
    row = lax.broadcasted_iota(jnp.int32, shape, 0)
    col = lax.broadcasted_iota(jnp.int32, shape, 1)
    own = (col & (DIFF_HEADS - 1)) == (row & (DIFF_HEADS - 1))
    s = _dot_nt(q8, k_s[...])
    last = jnp.where(g == n_steps - 1, bias_ref[0], 0.0)
    split = (n_pages_step - 1) * page_rows
    s = jnp.concatenate([s[:, :split], s[:, split:] + last], axis=1) if split else s + last
    s = jnp.where(own, s, NEG_INF)
    m, l, acc = _softmax_step(s, v_s[...], m_s[...], l_s[...], acc_s[...])
    m_s[...] = m
    l_s[...] = l
    acc_s[...] = acc

    @pl.when(g == n_steps - 1)
    def _():
        k_new = _rows_per_head(kn_ref[...]).astype(BF16).astype(F32)
        v_new = _rows_per_head(vn_ref[...]).astype(BF16).astype(F32)
        s_new = jnp.sum(q8.astype(F32) * k_new, axis=1, keepdims=True) + bias_ref[1][:, 0:1]
        m_new = jnp.maximum(m, s_new)
        a = jnp.exp(m - m_new)
        p_new = jnp.exp(s_new - m_new)
        l_f = a * l + p_new
        acc_f = a * acc + p_new.astype(BF16).astype(F32) * v_new
        d = acc_f / l_f
        lam = _diff_lambda(lam_ref[...], lam_init)
        o = d[0:DIFF_HEADS] - lam * d[DIFF_HEADS:n_rows]
        o_ref[...] = _sub_layer_norm(o, g_ref[...], lam_init).astype(BF16)


def _diff_sample(q12, k_new, v_new, cache_k, cache_v, page_table, bias_s, lam_params, subln_g, layer, lam_init):
    db = q12.shape[0]
    n_pages = page_table.shape[1]
    pps = min(DEC_PAGES, n_pages)
    n_steps = n_pages // pps
    n_rows = 2 * DIFF_HEADS
    width = DIFF_HEADS * PAGE_SIZE

    def page_spec(p):
        return pl.BlockSpec((None, None, width, LANES),
                            lambda b, g, pt: (layer, pt[b, g * pps + p], 0, 0))

    vec = pl.BlockSpec((None, 1, width), lambda b, g, pt: (b, 0, 0))
    grid_spec = pltpu.PrefetchScalarGridSpec(
        num_scalar_prefetch=1, grid=(db, n_steps),
        in_specs=[pl.BlockSpec((None, 4, 64), lambda b, g, pt: (layer, 0, 0)),
                  pl.BlockSpec((None, 1, DIFF_V_DIM), lambda b, g, pt: (layer, 0, 0)),
                  pl.BlockSpec((2, n_rows, width), lambda b, g, pt: (0, 0, 0)),
                  vec, vec, vec] + [page_spec(p) for p in range(pps)] * 2,
        out_specs=pl.BlockSpec((None, DIFF_HEADS, DIFF_V_DIM), lambda b, g, pt: (b, 0, 0)),
        scratch_shapes=[pltpu.VMEM((pps * width, LANES), BF16), pltpu.VMEM((pps * width, LANES), BF16),
                        pltpu.VMEM((n_rows, 1), F32), pltpu.VMEM((n_rows, 1), F32),
                        pltpu.VMEM((n_rows, DIFF_V_DIM), F32)])
    body = functools.partial(_diff_sample_body, n_pages_step=pps, n_steps=n_steps, lam_init=lam_init)
    out = pl.pallas_call(
        body, grid_spec=grid_spec, out_shape=jax.ShapeDtypeStruct((db, DIFF_HEADS, DIFF_V_DIM), BF16),
        compiler_params=_params(("arbitrary", "arbitrary")), name=f"diff_sample_{layer}")(
            page_table, lam_params, subln_g, bias_s, q12.reshape(db, 1, width), k_new.reshape(db, 1, width),
            v_new.reshape(db, 1, width), *([cache_k] * pps), *([cache_v] * pps))
    return out.reshape(db, DIFF_V_WIDTH)


ROUTE_E1, ROUTE_E2, ROUTE_RANK1, ROUTE_RANK2, ROUTE_W1, ROUTE_W2 = range(6)


def _to_row_tiles(ref, x):
    for c in range(x.shape[1] // LANES):
        ref[:, c, :] = x[:, c * LANES:(c + 1) * LANES]


def _from_row_tiles(ref):
    return jnp.concatenate([ref[:, c, :] for c in range(ref.shape[1])], axis=1)


def _merge_body(*refs, alpha, mode):
    if mode == "routed":
        (osb_ref, od_ref, gate_ref, x_ref, wpa_ref, wpb_ref, wo_ref, g_ref, b_ref, wr_ref,
         o_ref, xt_ref, route_ref, counts_ref, cnt_s) = refs
    elif mode == "gated":
        osb_ref, od_ref, gate_ref, x_ref, wpa_ref, wpb_ref, wo_ref, g_ref, b_ref, wr_ref, o_ref, gates_ref = refs
    else:
        osb_ref, od_ref, gate_ref, x_ref, wpa_ref, wpb_ref, wo_ref, g_ref, b_ref, o_ref = refs
    d = x_ref.shape[1]
    pa = _dot(osb_ref[...], wpa_ref[...])
    pb = _dot(od_ref[...], wpb_ref[...])
    mixed = gate_ref[:, 0:d].astype(F32) * pa + gate_ref[:, d:2 * d].astype(F32) * pb
    mix = _dot(mixed.astype(BF16), wo_ref[...])
    x1 = _layer_norm(alpha * x_ref[...] + mix, g_ref[...], b_ref[...])
    o_ref[...] = x1
    if mode == "plain":
        return
    logits = _dot(x1.astype(BF16), wr_ref[...])
    lane_i = lax.broadcasted_iota(jnp.int32, logits.shape, 1)
    logits = jnp.where(lane_i < N_EXPERTS, logits, -jnp.inf)
    lane = lane_i.astype(F32)
    v1 = jnp.max(logits, axis=1, keepdims=True)
    i1 = jnp.min(jnp.where(logits == v1, lane, float(LANES)), axis=1, keepdims=True)
    rest = jnp.where(lane == i1, -jnp.inf, logits)
    v2 = jnp.max(rest, axis=1, keepdims=True)
    i2 = jnp.min(jnp.where(rest == v2, lane, float(LANES)), axis=1, keepdims=True)
    e = jnp.exp(v2 - v1)
    w1 = 1.0 / (1.0 + e)
    w2 = e / (1.0 + e)
    if mode == "gated":
        gates_ref[...] = jnp.where(lane == i1, w1, 0.0) + jnp.where(lane == i2, w2, 0.0)
        return
    _to_row_tiles(xt_ref, x1)

    @pl.when(pl.program_id(0) == 0)
    def _():
        cnt_s[...] = jnp.zeros_like(cnt_s)

    tm = logits.shape[0]
    sel = jnp.where(lane == i1, 1.0, 0.0) + jnp.where(lane == i2, 1.0, 0.0)
    earlier = (lax.broadcasted_iota(jnp.int32, (tm, tm), 1) < lax.broadcasted_iota(jnp.int32, (tm, tm), 0))
    before = _dot(jnp.where(earlier, 1.0, 0.0).astype(BF16), sel.astype(BF16)) + cnt_s[...]
    rank1 = jnp.sum(jnp.where(lane == i1, before, 0.0), axis=1, keepdims=True)
    rank2 = jnp.sum(jnp.where(lane == i2, before, 0.0), axis=1, keepdims=True)
    cnt_s[...] += jnp.sum(sel, axis=0, keepdims=True)
    route = jnp.zeros_like(logits)
    for idx, val in ((ROUTE_E1, i1), (ROUTE_E2, i2), (ROUTE_RANK1, rank1), (ROUTE_RANK2, rank2),
                     (ROUTE_W1, w1), (ROUTE_W2, w2)):
        route = jnp.where(lane_i == idx, val, route)
    route_ref[...] = route
    counts_ref[...] = jnp.broadcast_to(cnt_s[...], counts_ref.shape)


def _merge(o_sb, o_d, gate, x, w_pa, w_pb, w_o, ln_g, ln_b, layer, alpha, mode="plain", router_w=None,
           moe_index=None):
    n, d = x.shape
    tm = min(ROW_TILE, n)
    row = lambda i: (i, 0)
    wsel = lambda i: (layer, 0, 0)
    vsel = lambda i: (layer, 0, 0)
    in_specs = [pl.BlockSpec((tm, SB_WIDTH), row), pl.BlockSpec((tm, DIFF_V_WIDTH), row),
                pl.BlockSpec((tm, 2 * d), row), pl.BlockSpec((tm, d), row),
                pl.BlockSpec((None, SB_WIDTH, d), wsel), pl.BlockSpec((None, DIFF_V_WIDTH, d), wsel),
                pl.BlockSpec((None, d, d), wsel), pl.BlockSpec((None, 1, d), vsel), pl.BlockSpec((None, 1, d), vsel)]
    args = [o_sb, o_d, gate, x, w_pa, w_pb, w_o, ln_g, ln_b]
    out_shape = [jax.ShapeDtypeStruct((n, d), F32)]
    out_specs = [pl.BlockSpec((tm, d), row)]
    scratch = []
    if mode != "plain":
        in_specs.append(pl.BlockSpec((None, d, LANES), lambda i: (moe_index, 0, 0)))
        args.append(router_w)
    if mode == "gated":
        out_shape.append(jax.ShapeDtypeStruct((n, LANES), F32))
        out_specs.append(pl.BlockSpec((tm, LANES), row))
    if mode == "routed":
        out_shape += [jax.ShapeDtypeStruct((n, d // LANES, LANES), F32), jax.ShapeDtypeStruct((n, LANES), F32),
                      jax.ShapeDtypeStruct((SUBLANES, LANES), F32)]
        out_specs += [pl.BlockSpec((tm, d // LANES, LANES), lambda i: (i, 0, 0)), pl.BlockSpec((tm, LANES), row),
                      pl.BlockSpec((SUBLANES, LANES), lambda i: (0, 0))]
        scratch = [pltpu.VMEM((1, LANES), F32)]
    body = functools.partial(_merge_body, alpha=alpha, mode=mode)
    return pl.pallas_call(
        body, grid=(n // tm,), in_specs=in_specs, out_specs=out_specs, out_shape=out_shape,
        scratch_shapes=scratch, compiler_params=_params(("arbitrary",)), name=f"merge_{layer}_{n}")(*args)


def _swiglu_hidden(xb, wg, wu):
    hg = _dot(xb, wg)
    hu = _dot(xb, wu)
    return (hg * jax.nn.sigmoid(hg) * hu).astype(BF16)


def _ffn_body(x_ref, wg_ref, wu_ref, wd_ref, g_ref, b_ref, o_ref, acc_s, xb_s, *, alpha, n_ff):
    j = pl.program_id(1)

    @pl.when(j == 0)
    def _():
        acc_s[...] = jnp.zeros_like(acc_s)
        xb_s[...] = x_ref[...].astype(BF16)

    h = _swiglu_hidden(xb_s[...], wg_ref[...], wu_ref[...])
    acc_s[...] += _dot(h, wd_ref[...])

    @pl.when(j == n_ff - 1)
    def _():
        o_ref[...] = _layer_norm(alpha * x_ref[...] + acc_s[...], g_ref[...], b_ref[...])


def _ffn_dense(x, wg, wu, wd, ln_g, ln_b, layer, index, alpha):
    n, d = x.shape
    f = wg.shape[2]
    tm = min(FFN_ROW_TILE, n)
    tf = _largest_tile(f, 256)
    n_ff = f // tf
    body = functools.partial(_ffn_body, alpha=alpha, n_ff=n_ff)
    return pl.pallas_call(
        body, grid=(n // tm, n_ff),
        in_specs=[pl.BlockSpec((tm, d), lambda i, j: (i, 0)),
                  pl.BlockSpec((None, d, tf), lambda i, j: (index, 0, j)),
                  pl.BlockSpec((None, d, tf), lambda i, j: (index, 0, j)),
                  pl.BlockSpec((None, tf, d), lambda i, j: (index, j, 0)),
                  pl.BlockSpec((None, 1, d), lambda i, j: (layer, 0, 0)),
                  pl.BlockSpec((None, 1, d), lambda i, j: (layer, 0, 0))],
        out_specs=pl.BlockSpec((tm, d), lambda i, j: (i, 0)),
        out_shape=jax.ShapeDtypeStruct((n, d), F32),
        scratch_shapes=[pltpu.VMEM((tm, d), F32), pltpu.VMEM((tm, d), BF16)],
        compiler_params=_params(("arbitrary", "arbitrary")), name=f"ffn_{layer}_{n}")(x, wg, wu, wd, ln_g, ln_b)


def _moe_body(x_ref, gates_ref, wg_ref, wu_ref, wd_ref, g_ref, b_ref, o_ref, acc_s, xb_s, *, alpha, n_ff):
    e = pl.program_id(1)
    j = pl.program_id(2)

    @pl.when(jnp.logical_and(e == 0, j == 0))
    def _():
        acc_s[...] = jnp.zeros_like(acc_s)
        xb_s[...] = x_ref[...].astype(BF16)

    h = _swiglu_hidden(xb_s[...], wg_ref[...].astype(BF16), wu_ref[...].astype(BF16))
    y = _dot(h, wd_ref[...].astype(BF16))
    gates = gates_ref[...]
    lane = lax.broadcasted_iota(jnp.int32, gates.shape, 1)
    gate_e = jnp.sum(jnp.where(lane == e, gates, 0.0), axis=1, keepdims=True)
    acc_s[...] += gate_e * y

    @pl.when(jnp.logical_and(e == N_EXPERTS - 1, j == n_ff - 1))
    def _():
        o_ref[...] = _layer_norm(alpha * x_ref[...] + acc_s[...], g_ref[...], b_ref[...])


def _ffn_moe(x, gates, wg, wu, wd, ln_g, ln_b, layer, index, alpha):
    n, d = x.shape
    f = wg.shape[3]
    tm = min(FFN_ROW_TILE, n)
    tf = _largest_tile(f, 256)
    n_ff = f // tf
    body = functools.partial(_moe_body, alpha=alpha, n_ff=n_ff)
    return pl.pallas_call(
        body, grid=(n // tm, N_EXPERTS, n_ff),
        in_specs=[pl.BlockSpec((tm, d), lambda i, e, j: (i, 0)),
                  pl.BlockSpec((tm, LANES), lambda i, e, j: (i, 0)),
                  pl.BlockSpec((None, None, d, tf), lambda i, e, j: (index, e, 0, j)),
                  pl.BlockSpec((None, None, d, tf), lambda i, e, j: (index, e, 0, j)),
                  pl.BlockSpec((None, None, tf, d), lambda i, e, j: (index, e, j, 0)),
                  pl.BlockSpec((None, 1, d), lambda i, e, j: (layer, 0, 0)),
                  pl.BlockSpec((None, 1, d), lambda i, e, j: (layer, 0, 0))],
        out_specs=pl.BlockSpec((tm, d), lambda i, e, j: (i, 0)),
        out_shape=jax.ShapeDtypeStruct((n, d), F32),
        scratch_shapes=[pltpu.VMEM((tm, d), F32), pltpu.VMEM((tm, d), BF16)],
        compiler_params=_params(("arbitrary", "arbitrary", "arbitrary")),
        name=f"moe_{layer}_{n}")(x, gates, wg, wu, wd, ln_g, ln_b)


def _routing_tables(route, counts, row_tile, tok_tile):
    n = route.shape[0]
    as_int = lambda lane: route[:, lane].astype(jnp.int32)
    e1, e2, rank1, rank2 = (as_int(k) for k in (ROUTE_E1, ROUTE_E2, ROUTE_RANK1, ROUTE_RANK2))
    cnt = counts[0, :N_EXPERTS].astype(jnp.int32)
    padded = (cnt + row_tile - 1) // row_tile * row_tile
    ends = jnp.cumsum(padded)
    starts = ends - padded
    pos1 = (jnp.take(starts, e1) + rank1).reshape(n // tok_tile, 1, tok_tile)
    pos2 = (jnp.take(starts, e2) + rank2).reshape(n // tok_tile, 1, tok_tile)
    pos = jnp.concatenate([pos1, pos2], axis=2)
    n_row_tiles = TOP_K * n // row_tile + N_EXPERTS
    tile_start = jnp.arange(n_row_tiles, dtype=jnp.int32) * row_tile
    tile_expert = jnp.minimum(jnp.sum(tile_start[:, None] >= ends[None, :], axis=1), N_EXPERTS - 1)
    tile_used = (tile_start < ends[N_EXPERTS - 1]).astype(jnp.int32)
    return pos, tile_expert.astype(jnp.int32), tile_used, n_row_tiles


def _row_copy_wait(src_hbm, dst, sem):
    pltpu.make_async_copy(src_hbm.at[0], dst.at[0], sem).wait()


def _dispatch_body(pos_ref, xt_hbm, rows_in, rows_hbm, sem, *, tok_tile):
    del rows_in
    base = pl.program_id(0) * tok_tile

    def issue(t, carry):
        src = xt_hbm.at[base + t]
        pltpu.make_async_copy(src, rows_hbm.at[pos_ref[0, t]], sem).start()
        pltpu.make_async_copy(src, rows_hbm.at[pos_ref[0, tok_tile + t]], sem).start()
        return carry

    lax.fori_loop(0, tok_tile, issue, 0)

    def drain(t, carry):
        _row_copy_wait(xt_hbm, rows_hbm, sem)
        _row_copy_wait(xt_hbm, rows_hbm, sem)
        return carry

    lax.fori_loop(0, tok_tile, drain, 0)


def _dispatch(xt, pos, n_rows, tok_tile):
    n = xt.shape[0]
    rows0 = jnp.zeros((n_rows,) + xt.shape[1:], F32)
    return pl.pallas_call(
        functools.partial(_dispatch_body, tok_tile=tok_tile), grid=(n // tok_tile,),
        in_specs=[pl.BlockSpec((None, 1, 2 * tok_tile), lambda i: (i, 0, 0), memory_space=pltpu.SMEM),
                  pl.BlockSpec(memory_space=pl.ANY), pl.BlockSpec(memory_space=pl.ANY)],
        out_specs=pl.BlockSpec(memory_space=pl.ANY),
        out_shape=jax.ShapeDtypeStruct(rows0.shape, F32),
        scratch_shapes=[pltpu.SemaphoreType.DMA(())],
        input_output_aliases={2: 0},
        compiler_params=_params(("arbitrary",)), name="moe_dispatch")(pos, xt, rows0)


def _expert_rows_body(te_ref, tu_ref, x_ref, wg_ref, wu_ref, wd_ref, y_ref, acc_s, xb_s, *, n_ff):
    del te_ref
    t = pl.program_id(0)
    j = pl.program_id(1)
    used = tu_ref[t] == 1

    @pl.when(j == 0)
    def _():
        acc_s[...] = jnp.zeros_like(acc_s)
        xb_s[...] = _from_row_tiles(x_ref).astype(BF16)

    @pl.when(used)
    def _():
        h = _swiglu_hidden(xb_s[...], wg_ref[...].astype(BF16), wu_ref[...].astype(BF16))
        acc_s[...] += _dot(h, wd_ref[...].astype(BF16))

    @pl.when(j == n_ff - 1)
    def _():
        _to_row_tiles(y_ref, acc_s[...])


def _expert_rows(rows, tile_expert, tile_used, wg, wu, wd, index, row_tile, n_row_tiles):
    d = rows.shape[1] * rows.shape[2]
    f = wg.shape[3]
    tf = _largest_tile(f, 256)
    n_ff = f // tf
    wcol = lambda t, j, te, tu: (index, te[t], 0, j * tu[t])
    wrow = lambda t, j, te, tu: (index, te[t], j * tu[t], 0)
    rmap = lambda t, j, te, tu: (t, 0, 0)
    grid_spec = pltpu.PrefetchScalarGridSpec(
        num_scalar_prefetch=2, grid=(n_row_tiles, n_ff),
        in_specs=[pl.BlockSpec((row_tile,) + rows.shape[1:], rmap),
                  pl.BlockSpec((None, None, d, tf), wcol), pl.BlockSpec((None, None, d, tf), wcol),
                  pl.BlockSpec((None, None, tf, d), wrow)],
        out_specs=pl.BlockSpec((row_tile,) + rows.shape[1:], rmap),
        scratch_shapes=[pltpu.VMEM((row_tile, d), F32), pltpu.VMEM((row_tile, d), BF16)])
    return pl.pallas_call(
        functools.partial(_expert_rows_body, n_ff=n_ff), grid_spec=grid_spec,
        out_shape=jax.ShapeDtypeStruct(rows.shape, F32),
        compiler_params=_params(("arbitrary", "arbitrary")), name="moe_experts")(
            tile_expert, tile_used, rows, wg, wu, wd)


def _combine_body(pos_ref, y_hbm, x_ref, route_ref, g_ref, b_ref, o_ref, buf, sem, *, tok_tile, alpha):
    def issue(t, carry):
        pltpu.make_async_copy(y_hbm.at[pos_ref[0, t]], buf.at[0, t], sem).start()
        pltpu.make_async_copy(y_hbm.at[pos_ref[0, tok_tile + t]], buf.at[1, t], sem).start()
        return carry

    lax.fori_loop(0, tok_tile, issue, 0)

    def drain(t, carry):
        _row_copy_wait(y_hbm, buf.at[0], sem)
        _row_copy_wait(y_hbm, buf.at[1], sem)
        return carry

    lax.fori_loop(0, tok_tile, drain, 0)
    route = route_ref[...]
    lane = lax.broadcasted_iota(jnp.int32, route.shape, 1)
    w1 = jnp.sum(jnp.where(lane == ROUTE_W1, route, 0.0), axis=1, keepdims=True)
    w2 = jnp.sum(jnp.where(lane == ROUTE_W2, route, 0.0), axis=1, keepdims=True)
    y = w1 * _from_row_tiles(buf.at[0]) + w2 * _from_row_tiles(buf.at[1])
    o_ref[...] = _layer_norm(alpha * x_ref[...] + y, g_ref[...], b_ref[...])


def _combine(y_rows, pos, x, route, ln_g, ln_b, layer, alpha, tok_tile):
    n, d = x.shape
    row = lambda i: (i, 0)
    return pl.pallas_call(
        functools.partial(_combine_body, tok_tile=tok_tile, alpha=alpha), grid=(n // tok_tile,),
        in_specs=[pl.BlockSpec((None, 1, 2 * tok_tile), lambda i: (i, 0, 0), memory_space=pltpu.SMEM),
                  pl.BlockSpec(memory_space=pl.ANY),
                  pl.BlockSpec((tok_tile, d), row), pl.BlockSpec((tok_tile, LANES), row),
                  pl.BlockSpec((None, 1, d), lambda i: (layer, 0, 0)),
                  pl.BlockSpec((None, 1, d), lambda i: (layer, 0, 0))],
        out_specs=pl.BlockSpec((tok_tile, d), row),
        out_shape=jax.ShapeDtypeStruct((n, d), F32),
        scratch_shapes=[pltpu.VMEM((TOP_K, tok_tile) + y_rows.shape[1:], F32), pltpu.SemaphoreType.DMA(())],
        compiler_params=_params(("arbitrary",)), name=f"moe_combine_{layer}")(pos, y_rows, x, route, ln_g, ln_b)


def _ffn_moe_routed(x, xt, route, counts, wg, wu, wd, ln_g, ln_b, layer, index, alpha):
    n = x.shape[0]
    row_tile = min(MOE_ROW_TILE, n)
    tok_tile = min(MOE_TOKEN_TILE, n)
    pos, tile_expert, tile_used, n_row_tiles = _routing_tables(route, counts, row_tile, tok_tile)
    rows = _dispatch(xt, pos, n_row_tiles * row_tile, tok_tile)
    y_rows = _expert_rows(rows, tile_expert, tile_used, wg, wu, wd, index, row_tile, n_row_tiles)
    return _combine(y_rows, pos, x, route, ln_g, ln_b, layer, alpha, tok_tile)


def _prepare_in_proj(w_in):
    depth, d, _ = w_in.shape

    def pair(a, b):
        a = a.reshape(depth, d, DIFF_HEADS, 64)
        b = b.reshape(depth, d, DIFF_HEADS, 64)
        return jnp.concatenate([a, b], axis=-1).reshape(depth, d, 2 * DIFF_QK_WIDTH)

    c = [0, 512, 1024, 1536, 1792, 2048, 2304, 2560, 3072, 3072 + 2 * d]
    part = [w_in[:, :, c[k]:c[k + 1]] for k in range(9)]
    q_sb, k_sb, v_sb, q1, q2, k1, k2, v_d, gate = part
    w_main = jnp.concatenate([q_sb, pair(q1, q2), pair(k1, k2), v_d, gate], axis=-1).astype(BF16)
    w_kvt = jnp.swapaxes(jnp.concatenate([k_sb, v_sb], axis=-1), 1, 2).astype(BF16)
    return w_main, w_kvt


def kernel(x_prompt, x_sample, cache_sb_k, cache_sb_v, cache_diff_k, cache_diff_v, page_table, rel_bias, w_in,
           w_pa, w_pb, w_o, diff_lambda, diff_subln_g, ln1_g, ln1_b, ln2_g, ln2_b, ffn_w_gate, ffn_w_up,
           ffn_w_down, router_w, moe_w_gate, moe_w_up, moe_w_down):
    batch, seq, d = x_prompt.shape
    db = x_sample.shape[0]
    depth = w_in.shape[0]
    n_pool = cache_sb_k.shape[1]
    alpha = (2.0 * depth) ** 0.25
    n = batch * seq

    w_main, w_kvt = _prepare_in_proj(w_in)
    w_pa_b, w_pb_b, w_o_b = w_pa.astype(BF16), w_pb.astype(BF16), w_o.astype(BF16)
    ffn_g_b, ffn_u_b, ffn_d_b = ffn_w_gate.astype(BF16), ffn_w_up.astype(BF16), ffn_w_down.astype(BF16)
    router_b = jnp.pad(router_w, ((0, 0), (0, 0), (0, LANES - N_EXPERTS))).astype(BF16)

    sbk_t = jnp.transpose(cache_sb_k, (0, 1, 3, 4, 2)).reshape(depth, n_pool, SB_WIDTH, PAGE_SIZE)
    sbv_t = jnp.transpose(cache_sb_v, (0, 1, 3, 4, 2)).reshape(depth, n_pool, SB_WIDTH, PAGE_SIZE)
    dk_r = cache_diff_k.reshape(depth, n_pool, PAGE_SIZE * DIFF_HEADS, LANES)
    dv_r = cache_diff_v.reshape(depth, n_pool, PAGE_SIZE * DIFF_HEADS, LANES)

    ln1_g, ln1_b, ln2_g, ln2_b = (v.reshape(depth, 1, d) for v in (ln1_g, ln1_b, ln2_g, ln2_b))
    diff_subln_g = diff_subln_g.reshape(depth, 1, DIFF_V_DIM)

    bias_t = _bias_tiles(rel_bias, min(DIFF_TILE, seq))
    bias_s = _bias_sample(rel_bias)

    xp = x_prompt.reshape(n, d)
    xs = x_sample.reshape(db, d)
    stacked = None
    s_ksb, s_vsb, s_k12, s_vd = [], [], [], []
    for l in range(depth):
        lam_init = 0.8 - 0.6 * math.exp(-0.3 * l)
        is_moe = l % 2 == 1
        idx = l // 2

        q_sb, q12, gate, k12_all, vd_all, kt_all, vt_all = _proj_prompt(xp, w_main, w_kvt, l, stacked, batch, seq)
        stacked = (k12_all, vd_all, kt_all, vt_all)
        o_sb = _sb_prompt(q_sb, kt_all, vt_all, l, batch, seq)
        o_d = _diff_prompt(q12, k12_all, vd_all, bias_t, diff_lambda, diff_subln_g, l, batch, seq, lam_init)
        if is_moe:
            xp, xt, route, counts = _merge(o_sb, o_d, gate, xp, w_pa_b, w_pb_b, w_o_b, ln1_g, ln1_b, l, alpha,
                                           "routed", router_b, idx)
            xp = _ffn_moe_routed(xp, xt, route, counts, moe_w_gate, moe_w_up, moe_w_down, ln2_g, ln2_b, l, idx,
                                 alpha)
        else:
            (xp,) = _merge(o_sb, o_d, gate, xp, w_pa_b, w_pb_b, w_o_b, ln1_g, ln1_b, l, alpha)
            xp = _ffn_dense(xp, ffn_g_b, ffn_u_b, ffn_d_b, ln2_g, ln2_b, l, idx, alpha)

        sq_sb, sq12, sgate, sk12, svd, sksb, svsb = _proj_sample(xs, w_main, w_kvt, l)
        so_sb = _sb_sample(sq_sb, sbk_t, sbv_t, page_table, l)
        so_d = _diff_sample(sq12, sk12, svd, dk_r, dv_r, page_table, bias_s, diff_lambda, diff_subln_g, l, lam_init)
        if is_moe:
            xs, sgates = _merge(so_sb, so_d, sgate, xs, w_pa_b, w_pb_b, w_o_b, ln1_g, ln1_b, l, alpha,
                                "gated", router_b, idx)
            xs = _ffn_moe(xs, sgates, moe_w_gate, moe_w_up, moe_w_down, ln2_g, ln2_b, l, idx, alpha)
        else:
            (xs,) = _merge(so_sb, so_d, sgate, xs, w_pa_b, w_pb_b, w_o_b, ln1_g, ln1_b, l, alpha)
            xs = _ffn_dense(xs, ffn_g_b, ffn_u_b, ffn_d_b, ln2_g, ln2_b, l, idx, alpha)
        s_ksb.append(sksb)
        s_vsb.append(svsb)
        s_k12.append(sk12)
        s_vd.append(svd)

    k12_all, vd_all, kt_all, vt_all = stacked
    to_sb = lambda t: jnp.transpose(t.reshape(depth, batch, SB_HEADS, 64, seq), (0, 1, 4, 2, 3))
    to_diff = lambda t: t.reshape(depth, batch, seq, DIFF_HEADS, DIFF_V_DIM)
    return (xp.reshape(batch, seq, d), xs.reshape(db, 1, d),
            to_sb(kt_all), to_sb(vt_all), to_diff(k12_all), to_diff(vd_all),
            jnp.stack(s_ksb).reshape(depth, db, 1, SB_HEADS, 64),
            jnp.stack(s_vsb).reshape(depth, db, 1, SB_HEADS, 64),
            jnp.stack(s_k12).reshape(depth, db, 1, DIFF_HEADS, DIFF_V_DIM),
            jnp.stack(s_vd).reshape(depth, db, 1, DIFF_HEADS, DIFF_V_DIM))
```

```python
import functools
import math

import jax
import jax.numpy as jnp
from jax import lax
from jax.experimental import pallas as pl
from jax.experimental.pallas import tpu as pltpu

F32 = jnp.float32
BF16 = jnp.bfloat16

PAGE_SIZE = 128
SB_HEADS = 8
SB_WIDTH = 512
DIFF_HEADS = 4
DIFF_V_DIM = 128
DIFF_QK_WIDTH = 256
DIFF_V_WIDTH = 512
N_EXPERTS = 8
TOP_K = 2
N_BUCKETS = 32
MAX_DISTANCE = 128
LN_EPS = 1e-5
SUBLN_EPS = 1e-5
NEG_INF = -1e30
QK_SCALE = 0.125

LANES = 128
SUBLANES = 8
VMEM_LIMIT_BYTES = 56 * 1024 * 1024

ROW_TILE = 512
FFN_ROW_TILE = 1024
DIFF_TILE = 512
SB_TILE_Q = 512
SB_TILE_K = 512
DEC_PAGES = 8
SB_HEADS_PER_STEP = 4
DIFF_HEADS_PER_STEP = 2
MOE_ROW_TILE = 1024
MOE_TOKEN_TILE = 256

NT_DIMS = (((1,), (1,)), ((), ()))


def _params(semantics):
    return pltpu.CompilerParams(dimension_semantics=semantics, vmem_limit_bytes=VMEM_LIMIT_BYTES)


def _dot(a, b):
    return jnp.dot(a, b, preferred_element_type=F32)


def _dot_nt(a, b):
    return lax.dot_general(a, b, NT_DIMS, preferred_element_type=F32)


def _layer_norm(y, g, b):
    mu = jnp.mean(y, axis=-1, keepdims=True)
    d = y - mu
    var = jnp.mean(d * d, axis=-1, keepdims=True)
    return d * lax.rsqrt(var + LN_EPS) * g + b


def _strict_lower_ones(n):
    j = lax.broadcasted_iota(jnp.int32, (n, n), 0)
    s = lax.broadcasted_iota(jnp.int32, (n, n), 1)
    return jnp.where(j > s, 1.0, 0.0).astype(BF16)


def _largest_tile(dim, cap):
    if dim <= cap:
        return dim
    best = None
    for t in range(LANES, cap + 1, LANES):
        if dim % t == 0:
            best = t
    assert best is not None, (dim, cap)
    return best


def _store_head_rows(ref, x):
    rows = x.shape[0]
    for hd in range(DIFF_HEADS):
        ref[pl.ds(hd, rows, stride=DIFF_HEADS), :] = x[:, hd * LANES:(hd + 1) * LANES]


def _proj_prompt_body(x_ref, wm_ref, wkvt_ref, *rest):
    qsb_ref, q12_ref, gate_ref, k12_ref, vd_ref, ksbt_ref, vsbt_ref = rest[-7:]
    x = x_ref[...].astype(BF16)

    def cols(c0, n):
        return _dot(x, wm_ref[:, c0:c0 + n])

    qsb_ref[...] = (cols(0, 512) * QK_SCALE).astype(BF16)
    q12_ref[...] = (cols(512, 512) * QK_SCALE).astype(BF16)
    _store_head_rows(k12_ref, cols(1024, 512))
    _store_head_rows(vd_ref, cols(1536, 512))
    gate_ref[...] = jax.nn.sigmoid(cols(2048, 2048)).astype(BF16)
    ksbt_ref[...] = _dot_nt(wkvt_ref[0:512, :], x)
    vsbt_ref[...] = _dot_nt(wkvt_ref[512:1024, :], x)


def _proj_prompt(x, w_main, w_kvt, layer, stacked, batch, seq):
    n, d = x.shape
    depth = w_main.shape[0]
    tm = min(ROW_TILE, seq)
    nt = seq // tm
    grid = (n // tm,)
    row = lambda i: (i, 0)
    in_specs = [
        pl.BlockSpec((tm, d), row),
        pl.BlockSpec((None, d, w_main.shape[2]), lambda i: (layer, 0, 0)),
        pl.BlockSpec((None, w_kvt.shape[1], d), lambda i: (layer, 0, 0)),
    ]
    args = [x, w_main, w_kvt]
    aliases = {}
    if stacked is not None:
        in_specs += [pl.BlockSpec(memory_space=pl.ANY)] * 4
        args += list(stacked)
        aliases = {3: 3, 4: 4, 5: 5, 6: 6}
    out_shape = [
        jax.ShapeDtypeStruct((n, 512), BF16),
        jax.ShapeDtypeStruct((n, 512), BF16),
        jax.ShapeDtypeStruct((n, 2048), BF16),
        jax.ShapeDtypeStruct((depth, DIFF_HEADS * n, LANES), F32),
        jax.ShapeDtypeStruct((depth, DIFF_HEADS * n, LANES), F32),
        jax.ShapeDtypeStruct((depth, batch, 512, seq), F32),
        jax.ShapeDtypeStruct((depth, batch, 512, seq), F32),
    ]
    out_specs = [
        pl.BlockSpec((tm, 512), row),
        pl.BlockSpec((tm, 512), row),
        pl.BlockSpec((tm, 2048), row),
        pl.BlockSpec((None, DIFF_HEADS * tm, LANES), lambda i: (layer, i, 0)),
        pl.BlockSpec((None, DIFF_HEADS * tm, LANES), lambda i: (layer, i, 0)),
        pl.BlockSpec((None, None, 512, tm), lambda i: (layer, i // nt, 0, i % nt)),
        pl.BlockSpec((None, None, 512, tm), lambda i: (layer, i // nt, 0, i % nt)),
    ]
    return pl.pallas_call(
        _proj_prompt_body, grid=grid, in_specs=in_specs, out_specs=out_specs, out_shape=out_shape,
        input_output_aliases=aliases, compiler_params=_params(("arbitrary",)),
        name=f"proj_prompt_{layer}")(*args)


def _proj_sample_body(x_ref, wm_ref, wkvt_ref, qsb_ref, q12_ref, gate_ref, k12_ref, vd_ref, ksb_ref, vsb_ref):
    x = x_ref[...].astype(BF16)

    def cols(c0, n):
        return _dot(x, wm_ref[:, c0:c0 + n])

    qsb_ref[...] = (cols(0, 512) * QK_SCALE).astype(BF16)
    q12_ref[...] = (cols(512, 512) * QK_SCALE).astype(BF16)
    k12_ref[...] = cols(1024, 512)
    vd_ref[...] = cols(1536, 512)
    gate_ref[...] = jax.nn.sigmoid(cols(2048, 2048)).astype(BF16)
    ksb_ref[...] = _dot_nt(x, wkvt_ref[0:512, :])
    vsb_ref[...] = _dot_nt(x, wkvt_ref[512:1024, :])


def _proj_sample(x, w_main, w_kvt, layer):
    n, d = x.shape
    full = lambda shape: pl.BlockSpec(shape, lambda i: (0,) * len(shape))
    out_shape = [
        jax.ShapeDtypeStruct((n, 512), BF16), jax.ShapeDtypeStruct((n, 512), BF16),
        jax.ShapeDtypeStruct((n, 2048), BF16),
        jax.ShapeDtypeStruct((n, 512), F32), jax.ShapeDtypeStruct((n, 512), F32),
        jax.ShapeDtypeStruct((n, 512), F32), jax.ShapeDtypeStruct((n, 512), F32),
    ]
    return pl.pallas_call(
        _proj_sample_body, grid=(1,),
        in_specs=[full((n, d)),
                  pl.BlockSpec((None, d, w_main.shape[2]), lambda i: (layer, 0, 0)),
                  pl.BlockSpec((None, w_kvt.shape[1], d), lambda i: (layer, 0, 0))],
        out_specs=[full(s.shape) for s in out_shape], out_shape=out_shape,
        compiler_params=_params(("arbitrary",)), name=f"proj_sample_{layer}")(x, w_main, w_kvt)


def _softplus(z):
    return jnp.maximum(z, 0.0) + jnp.log(1.0 + jnp.exp(-jnp.abs(z)))


def _neg_suffix_sums(sp, neg_tri):
    hi = sp.astype(BF16)
    lo = (sp - hi.astype(F32)).astype(BF16)
    return _dot(hi, neg_tri) + _dot(lo, neg_tri)


def _sb_block(q, k_t, v_t, neg_tri, r, acc, causal):
    z = _dot(q, k_t)
    sp = _softplus(z)
    log_beta = z - sp
    if causal is not None:
        sp = jnp.where(causal, sp, 0.0)
    cum = _neg_suffix_sums(sp, neg_tri)
    w = jnp.exp(log_beta + cum + r)
    if causal is not None:
        w = jnp.where(causal, w, 0.0)
    acc = acc + _dot_nt(w.astype(BF16), v_t)
    r = r + cum[:, 0:1] - sp[:, 0:1]
    return r, acc


def _sb_prompt_body(q_ref, kt_ref, vt_ref, o_ref, kt_s, vt_s, r_s, acc_s, *, tq, tk, n_blocks, n_pairs):
    i = pl.program_id(2)
    ratio = tq // tk

    @pl.when(i == 0)
    def _():
        for c in range(n_blocks):
            kt_s[c] = kt_ref[:, c * tk:(c + 1) * tk].astype(BF16)
            vt_s[c] = vt_ref[:, c * tk:(c + 1) * tk].astype(BF16)

    lane = lax.broadcasted_iota(jnp.int32, (tq, LANES), 1)
    q_heads = []
    for hp in range(n_pairs):
        q2 = q_ref[:, hp * LANES:(hp + 1) * LANES]
        zero = jnp.zeros_like(q2)
        q_heads += [jnp.where(lane < 64, q2, zero), jnp.where(lane >= 64, q2, zero)]
    row = lax.broadcasted_iota(jnp.int32, (tq, tk), 0)
    col = lax.broadcasted_iota(jnp.int32, (tq, tk), 1)
    neg_tri = -_strict_lower_ones(tk)
    r_s[...] = jnp.zeros_like(r_s)
    acc_s[...] = jnp.zeros_like(acc_s)

    def step(kb, mask):
        for h in range(2 * n_pairs):
            rows = slice((h // 2) * LANES, (h // 2 + 1) * LANES)
            r, acc = _sb_block(q_heads[h], kt_s[kb, rows, :], vt_s[kb, rows, :], neg_tri, r_s[h], acc_s[h], mask)
            r_s[h] = r
            acc_s[h] = acc

    for d in reversed(range(ratio)):
        step(i * ratio + d, col + d * tk < row)

    def body(j, carry):
        step(i * ratio - 1 - j, None)
        return carry

    lax.fori_loop(0, i * ratio, body, 0)
    for hp in range(n_pairs):
        o_ref[:, hp * LANES:(hp + 1) * LANES] = jnp.where(lane < 64, acc_s[2 * hp], acc_s[2 * hp + 1]).astype(BF16)


def _sb_prompt(q, kt_all, vt_all, layer, batch, seq):
    tq = min(SB_TILE_Q, seq)
    tk = min(SB_TILE_K, tq)
    nq = seq // tq
    n_pairs = SB_HEADS_PER_STEP // 2
    width = n_pairs * LANES
    body = functools.partial(_sb_prompt_body, tq=tq, tk=tk, n_blocks=seq // tk, n_pairs=n_pairs)
    qmap = lambda b, h, i: (b * nq + i, h)
    kvmap = lambda b, h, i: (layer, b, h, 0)
    return pl.pallas_call(
        body, grid=(batch, SB_HEADS // SB_HEADS_PER_STEP, nq),
        in_specs=[pl.BlockSpec((tq, width), qmap),
                  pl.BlockSpec((None, None, width, seq), kvmap),
                  pl.BlockSpec((None, None, width, seq), kvmap)],
        out_specs=pl.BlockSpec((tq, width), qmap),
        out_shape=jax.ShapeDtypeStruct((batch * seq, SB_WIDTH), BF16),
        scratch_shapes=[pltpu.VMEM((seq // tk, width, tk), BF16), pltpu.VMEM((seq // tk, width, tk), BF16),
                        pltpu.VMEM((SB_HEADS_PER_STEP, tq, 1), F32),
                        pltpu.VMEM((SB_HEADS_PER_STEP, tq, LANES), F32)],
        compiler_params=_params(("arbitrary", "arbitrary", "arbitrary")),
        name=f"sb_prompt_{layer}")(q, kt_all, vt_all)


def _sb_sample_body(pt_ref, q_ref, *rest, n_pages_step, n_steps):
    del pt_ref
    k_refs = rest[:n_pages_step]
    v_refs = rest[n_pages_step:2 * n_pages_step]
    o_ref, kt_s, vt_s, r_s, acc_s = rest[2 * n_pages_step:]
    g = pl.program_id(1)
    page = lambda p: slice(p * PAGE_SIZE, (p + 1) * PAGE_SIZE)

    @pl.when(g == 0)
    def _():
        r_s[...] = jnp.zeros_like(r_s)
        acc_s[...] = jnp.zeros_like(acc_s)

    for p in range(n_pages_step):
        kt_s[:, page(p)] = k_refs[p][...].astype(BF16)
        vt_s[:, page(p)] = v_refs[p][...].astype(BF16)

    q = q_ref[...]
    row = lax.broadcasted_iota(jnp.int32, (SB_HEADS, SB_WIDTH), 0)
    col = lax.broadcasted_iota(jnp.int32, (SB_HEADS, SB_WIDTH), 1)
    own = (col >> 6) == row
    q_bd = jnp.where(own, jnp.broadcast_to(q.astype(F32), (SB_HEADS, SB_WIDTH)), 0.0).astype(BF16)
    z = _dot(q_bd, kt_s[...])
    sp = _softplus(z)
    log_beta = z - sp
    sp_rows = jnp.concatenate([sp[:, page(p)] for p in range(n_pages_step)], axis=0)
    cum_rows = _neg_suffix_sums(sp_rows, -_strict_lower_ones(PAGE_SIZE))
    r = r_s[...]
    w = [None] * n_pages_step
    for p in reversed(range(n_pages_step)):
        cum = cum_rows[p * SB_HEADS:(p + 1) * SB_HEADS]
        sp_p = sp[:, page(p)]
        w[p] = jnp.exp(log_beta[:, page(p)] + cum + r)
        r = r + jnp.broadcast_to(cum[:, 0:1] - sp_p[:, 0:1], r.shape)
    r_s[...] = r
    acc = acc_s[...] + _dot_nt(jnp.concatenate(w, axis=1).astype(BF16), vt_s[...])
    acc_s[...] = acc

    @pl.when(g == n_steps - 1)
    def _():
        o_ref[...] = jnp.sum(jnp.where(own, acc, 0.0), axis=0, keepdims=True).astype(BF16)


def _sb_sample(q, cache_kt, cache_vt, page_table, layer):
    db = q.shape[0]
    n_pages = page_table.shape[1]
    pps = min(DEC_PAGES, n_pages)
    n_steps = n_pages // pps
    q3 = q.reshape(db, 1, SB_WIDTH)

    def page_spec(p):
        return pl.BlockSpec((None, None, SB_WIDTH, PAGE_SIZE),
                            lambda b, g, pt: (layer, pt[b, (n_steps - 1 - g) * pps + p], 0, 0))

    qspec = pl.BlockSpec((None, 1, SB_WIDTH), lambda b, g, pt: (b, 0, 0))
    grid_spec = pltpu.PrefetchScalarGridSpec(
        num_scalar_prefetch=1, grid=(db, n_steps),
        in_specs=[qspec] + [page_spec(p) for p in range(pps)] * 2,
        out_specs=qspec,
        scratch_shapes=[pltpu.VMEM((SB_WIDTH, pps * PAGE_SIZE), BF16), pltpu.VMEM((SB_WIDTH, pps * PAGE_SIZE), BF16),
                        pltpu.VMEM((SB_HEADS, LANES), F32), pltpu.VMEM((SB_HEADS, SB_WIDTH), F32)])
    body = functools.partial(_sb_sample_body, n_pages_step=pps, n_steps=n_steps)
    out = pl.pallas_call(
        body, grid_spec=grid_spec, out_shape=jax.ShapeDtypeStruct((db, 1, SB_WIDTH), BF16),
        compiler_params=_params(("arbitrary", "arbitrary")), name=f"sb_sample_{layer}")(
            page_table, q3, *([cache_kt] * pps), *([cache_vt] * pps))
    return out.reshape(db, SB_WIDTH)


def _t5_bucket(n):
    max_exact = N_BUCKETS // 2
    nf = jnp.maximum(n, 1).astype(F32)
    large = max_exact + (jnp.log(nf / max_exact) / math.log(MAX_DISTANCE / max_exact)
                         * (N_BUCKETS - max_exact)).astype(jnp.int32)
    large = jnp.minimum(large, N_BUCKETS - 1)
    return jnp.where(n < max_exact, n, large)


def _bias_lookup(bucket, head, rb_ref, n_heads_static):
    val = jnp.zeros(bucket.shape, F32)
    for k in range(N_BUCKETS):
        if isinstance(head, int):
            val = jnp.where(bucket == k, rb_ref[k, head], val)
        else:
            for hh in range(n_heads_static):
                val = jnp.where((bucket == k) & (head == hh), rb_ref[k, hh], val)
    return val


def _bias_tiles_body(rb_ref, o_ref, *, tile):
    h = pl.program_id(0)
    which = pl.program_id(1)
    row = lax.broadcasted_iota(jnp.int32, (tile, tile), 0)
    col = lax.broadcasted_iota(jnp.int32, (tile, tile), 1)
    n = jnp.maximum(row - col + which * tile, 0)
    bucket = _t5_bucket(n)
    val = jnp.zeros((tile, tile), F32)
    for k in range(N_BUCKETS):
        val = jnp.where(bucket == k, rb_ref[k, h], val)
    o_ref[...] = val - rb_ref[N_BUCKETS - 1, h]


def _bias_tiles(rel_bias, tile):
    return pl.pallas_call(
        functools.partial(_bias_tiles_body, tile=tile), grid=(DIFF_HEADS, 2),
        in_specs=[pl.BlockSpec(memory_space=pltpu.SMEM)],
        out_specs=pl.BlockSpec((None, None, tile, tile), lambda h, w: (h, w, 0, 0)),
        out_shape=jax.ShapeDtypeStruct((DIFF_HEADS, 2, tile, tile), F32),
        compiler_params=_params(("arbitrary", "arbitrary")), name="bias_tiles")(rel_bias)


def _bias_sample_body(rb_ref, o_ref):
    shape = (2 * DIFF_HEADS, DIFF_HEADS * PAGE_SIZE)
    head = lax.broadcasted_iota(jnp.int32, shape, 0) & (DIFF_HEADS - 1)
    pos = lax.broadcasted_iota(jnp.int32, shape, 1) >> 2
    far = jnp.zeros(shape, jnp.int32) + MAX_DISTANCE
    last = PAGE_SIZE - pos
    new = jnp.zeros(shape, jnp.int32)
    far_bias = _bias_lookup(_t5_bucket(far), head, rb_ref, DIFF_HEADS)
    for idx, n in enumerate((last, new)):
        o_ref[idx] = _bias_lookup(_t5_bucket(n), head, rb_ref, DIFF_HEADS) - far_bias


def _bias_sample(rel_bias):
    shape = (2, 2 * DIFF_HEADS, DIFF_HEADS * PAGE_SIZE)
    return pl.pallas_call(
        _bias_sample_body, grid=(1,),
        in_specs=[pl.BlockSpec(memory_space=pltpu.SMEM)],
        out_specs=pl.BlockSpec(shape, lambda i: (0, 0, 0)),
        out_shape=jax.ShapeDtypeStruct(shape, F32),
        compiler_params=_params(("arbitrary",)), name="bias_sample")(rel_bias)


def _diff_lambda(lp, lam_init):
    a = jnp.sum(lp[0:1] * lp[1:2], axis=1, keepdims=True)
    b = jnp.sum(lp[2:3] * lp[3:4], axis=1, keepdims=True)
    return jnp.exp(a) - jnp.exp(b) + lam_init


def _softmax_step(s, v, m, l, acc):
    m_new = jnp.maximum(m, jnp.max(s, axis=1, keepdims=True))
    a = jnp.exp(m - m_new)
    p = jnp.exp(s - m_new)
    l = a * l + jnp.sum(p, axis=1, keepdims=True)
    acc = a * acc + _dot(p.astype(BF16), v)
    return m_new, l, acc


def _sub_layer_norm(o, g, lam_init):
    o = o * lax.rsqrt(jnp.mean(o * o, axis=-1, keepdims=True) + SUBLN_EPS)
    return o * g * (1.0 - lam_init)


def _diff_prompt_body(lam_ref, g_ref, q_ref, k_ref, v_ref, d_ref, o_ref,
                      k_s, v_s, m_s, acc_s, *, tile, lam_init, n_heads):
    i = pl.program_id(2)
    head = lambda hd: slice(hd * LANES, (hd + 1) * LANES)
    seq = k_s.shape[0]

    @pl.when(i == 0)
    def _():
        for hd in range(n_heads):
            rows = pl.ds(pl.program_id(1) * n_heads + hd, seq, stride=DIFF_HEADS)
            k_s[:, head(hd)] = k_ref[rows, :].astype(BF16)
            v_s[hd, :, 0:LANES] = v_ref[rows, :].astype(BF16)
            v_s[hd, :, LANES:2 * LANES] = jnp.ones((seq, LANES), BF16)

    lane = lax.broadcasted_iota(jnp.int32, (tile, LANES), 1)
    q_maps = []
    for hd in range(n_heads):
        q2 = q_ref[:, head(hd)]
        zero = jnp.zeros_like(q2)
        q_maps += [jnp.where(lane < 64, q2, zero), jnp.where(lane >= 64, q2, zero)]
    row = lax.broadcasted_iota(jnp.int32, (tile, tile), 0)
    col = lax.broadcasted_iota(jnp.int32, (tile, tile), 1)
    visible = col <= row
    m_s[...] = jnp.full_like(m_s, NEG_INF)
    acc_s[...] = jnp.zeros_like(acc_s)

    def step(kb, near, mask):
        start = pl.multiple_of(kb * tile, tile)
        for c in range(2 * n_heads):
            hd = c // 2
            s = _dot_nt(q_maps[c], k_s[pl.ds(start, tile), head(hd)])
            if near is not None:
                s = s + d_ref[hd, near]
            if mask is not None:
                s = jnp.where(mask, s, NEG_INF)
            m_old = m_s[c]
            m_new = jnp.maximum(m_old, jnp.max(s, axis=1, keepdims=True))
            p = jnp.exp(s - m_new)
            acc_s[c] = jnp.exp(m_old - m_new) * acc_s[c] + _dot(p.astype(BF16), v_s[hd, pl.ds(start, tile), :])
            m_s[c] = m_new

    step(i, 0, visible)

    @pl.when(i >= 1)
    def _():
        step(i - 1, 1, None)

    def body(j, carry):
        step(j, None, None)
        return carry

    lax.fori_loop(0, i - 1, body, 0)
    lam = _diff_lambda(lam_ref[...], lam_init)
    for hd in range(n_heads):
        a1, a2 = acc_s[2 * hd], acc_s[2 * hd + 1]
        o = a1[:, 0:LANES] / a1[:, LANES:2 * LANES] - lam * (a2[:, 0:LANES] / a2[:, LANES:2 * LANES])
        o_ref[:, head(hd)] = _sub_layer_norm(o, g_ref[...], lam_init).astype(BF16)


def _diff_prompt(q12, k12_all, vd_all, bias_tiles, lam_params, subln_g, layer, batch, seq, lam_init):
    tile = min(DIFF_TILE, seq)
    assert tile >= MAX_DISTANCE
    nq = seq // tile
    nh = DIFF_HEADS_PER_STEP
    width = nh * LANES
    body = functools.partial(_diff_prompt_body, tile=tile, lam_init=lam_init, n_heads=nh)
    qmap = lambda b, h, i: (b * nq + i, h)
    kvmap = lambda b, h, i: (layer, b, 0)
    return pl.pallas_call(
        body, grid=(batch, DIFF_HEADS // nh, nq),
        in_specs=[pl.BlockSpec((None, 4, 64), lambda b, h, i: (layer, 0, 0)),
                  pl.BlockSpec((None, 1, DIFF_V_DIM), lambda b, h, i: (layer, 0, 0)),
                  pl.BlockSpec((tile, width), qmap),
                  pl.BlockSpec((None, DIFF_HEADS * seq, LANES), kvmap),
                  pl.BlockSpec((None, DIFF_HEADS * seq, LANES), kvmap),
                  pl.BlockSpec((nh, 2, tile, tile), lambda b, h, i: (h, 0, 0, 0))],
        out_specs=pl.BlockSpec((tile, width), qmap),
        out_shape=jax.ShapeDtypeStruct((batch * seq, DIFF_V_WIDTH), BF16),
        scratch_shapes=[pltpu.VMEM((seq, width), BF16), pltpu.VMEM((nh, seq, 2 * LANES), BF16),
                        pltpu.VMEM((2 * nh, tile, 1), F32), pltpu.VMEM((2 * nh, tile, 2 * LANES), F32)],
        compiler_params=_params(("arbitrary", "arbitrary", "arbitrary")),
        name=f"diff_prompt_{layer}")(lam_params, subln_g, q12, k12_all, vd_all, bias_tiles)


def _rows_per_head(x):
    n_rows = 2 * DIFF_HEADS
    head = lax.broadcasted_iota(jnp.int32, (n_rows, LANES), 0) & (DIFF_HEADS - 1)
    out = jnp.zeros((n_rows, LANES), F32)
    for hh in range(DIFF_HEADS):
        out = jnp.where(head == hh, jnp.broadcast_to(x[:, hh * LANES:(hh + 1) * LANES], (n_rows, LANES)), out)
    return out


def _diff_sample_body(pt_ref, lam_ref, g_ref, bias_ref, q_ref, kn_ref, vn_ref, *rest,
                      n_pages_step, n_steps, lam_init):
    del pt_ref
    k_refs = rest[:n_pages_step]
    v_refs = rest[n_pages_step:2 * n_pages_step]
    o_ref, k_s, v_s, m_s, l_s, acc_s = rest[2 * n_pages_step:]
    g = pl.program_id(1)
    n_rows = 2 * DIFF_HEADS
    page_rows = DIFF_HEADS * PAGE_SIZE

    @pl.when(g == 0)
    def _():
        m_s[...] = jnp.full_like(m_s, NEG_INF)
        l_s[...] = jnp.zeros_like(l_s)
        acc_s[...] = jnp.zeros_like(acc_s)

    for p in range(n_pages_step):
        k_s[p * page_rows:(p + 1) * page_rows, :] = k_refs[p][...].astype(BF16)
        v_s[p * page_rows:(p + 1) * page_rows, :] = v_refs[p][...].astype(BF16)

    q8 = _rows_per_head(q_ref[...].astype(F32))
    r8 = lax.broadcasted_iota(jnp.int32, (n_rows, LANES), 0)
    l8 = lax.broadcasted_iota(jnp.int32, (n_rows, LANES), 1)
    q8 = jnp.where((l8 >> 6) == (r8 >> 2), q8, 0.0).astype(BF16)
    shape = (n_rows, n_pages_step * page_rows)
    row = lax.broadcasted_iota(jnp.int32, shape, 0)
    col = lax.broadcasted_iota(jnp.int32, shape, 1)
    own = (col & (DIFF_HEADS - 1)) == (row & (DIFF_HEADS - 1))
    s = _dot_nt(q8, k_s[...])
    last = jnp.where(g == n_steps - 1, bias_ref[0], 0.0)
    split = (n_pages_step - 1) * page_rows
    s = jnp.concatenate([s[:, :split], s[:, split:] + last], axis=1) if split else s + last
    s = jnp.where(own, s, NEG_INF)
    m, l, acc = _softmax_step(s, v_s[...], m_s[...], l_s[...], acc_s[...])
    m_s[...] = m
    l_s[...] = l
    acc_s[...] = acc

    @pl.when(g == n_steps - 1)
    def _():
        k_new = _rows_per_head(kn_ref[...]).astype(BF16).astype(F32)
        v_new = _rows_per_head(vn_ref[...]).astype(BF16).astype(F32)
        s_new = jnp.sum(q8.astype(F32) * k_new, axis=1, keepdims=True) + bias_ref[1][:, 0:1]
        m_new = jnp.maximum(m, s_new)
        a = jnp.exp(m - m_new)
        p_new = jnp.exp(s_new - m_new)
        l_f = a * l + p_new
        acc_f = a * acc + p_new.astype(BF16).astype(F32) * v_new
        d = acc_f / l_f
        lam = _diff_lambda(lam_ref[...], lam_init)
        o = d[0:DIFF_HEADS] - lam * d[DIFF_HEADS:n_rows]
        o_ref[...] = _sub_layer_norm(o, g_ref[...], lam_init).astype(BF16)


def _diff_sample(q12, k_new, v_new, cache_k, cache_v, page_table, bias_s, lam_params, subln_g, layer, lam_init):
    db = q12.shape[0]
    n_pages = page_table.shape[1]
    pps = min(DEC_PAGES, n_pages)
    n_steps = n_pages // pps
    n_rows = 2 * DIFF_HEADS
    width = DIFF_HEADS * PAGE_SIZE

    def page_spec(p):
        return pl.BlockSpec((None, None, width, LANES),
                            lambda b, g, pt: (layer, pt[b, g * pps + p], 0, 0))

    vec = pl.BlockSpec((None, 1, width), lambda b, g, pt: (b, 0, 0))
    grid_spec = pltpu.PrefetchScalarGridSpec(
        num_scalar_prefetch=1, grid=(db, n_steps),
        in_specs=[pl.BlockSpec((None, 4, 64), lambda b, g, pt: (layer, 0, 0)),
                  pl.BlockSpec((None, 1, DIFF_V_DIM), lambda b, g, pt: (layer, 0, 0)),
                  pl.BlockSpec((2, n_rows, width), lambda b, g, pt: (0, 0, 0)),
                  vec, vec, vec] + [page_spec(p) for p in range(pps)] * 2,
        out_specs=pl.BlockSpec((None, DIFF_HEADS, DIFF_V_DIM), lambda b, g, pt: (b, 0, 0)),
        scratch_shapes=[pltpu.VMEM((pps * width, LANES), BF16), pltpu.VMEM((pps * width, LANES), BF16),
                        pltpu.VMEM((n_rows, 1), F32), pltpu.VMEM((n_rows, 1), F32),
                        pltpu.VMEM((n_rows, DIFF_V_DIM), F32)])
    body = functools.partial(_diff_sample_body, n_pages_step=pps, n_steps=n_steps, lam_init=lam_init)
    out = pl.pallas_call(
        body, grid_spec=grid_spec, out_shape=jax.ShapeDtypeStruct((db, DIFF_HEADS, DIFF_V_DIM), BF16),
        compiler_params=_params(("arbitrary", "arbitrary")), name=f"diff_sample_{layer}")(
            page_table, lam_params, subln_g, bias_s, q12.reshape(db, 1, width), k_new.reshape(db, 1, width),
            v_new.reshape(db, 1, width), *([cache_k] * pps), *([cache_v] * pps))
    return out.reshape(db, DIFF_V_WIDTH)


ROUTE_E1, ROUTE_E2, ROUTE_RANK1, ROUTE_RANK2, ROUTE_W1, ROUTE_W2 = range(6)


def _to_row_tiles(ref, x):
    for c in range(x.shape[1] // LANES):
        ref[:, c, :] = x[:, c * LANES:(c + 1) * LANES]


def _from_row_tiles(ref):
    return jnp.concatenate([ref[:, c, :] for c in range(ref.shape[1])], axis=1)


def _merge_body(*refs, alpha, mode):
    if mode == "routed":
        (osb_ref, od_ref, gate_ref, x_ref, wpa_ref, wpb_ref, wo_ref, g_ref, b_ref, wr_ref,
         o_ref, xt_ref, route_ref, counts_ref, cnt_s) = refs
    elif mode == "gated":
        osb_ref, od_ref, gate_ref, x_ref, wpa_ref, wpb_ref, wo_ref, g_ref, b_ref, wr_ref, o_ref, gates_ref = refs
    else:
        osb_ref, od_ref, gate_ref, x_ref, wpa_ref, wpb_ref, wo_ref, g_ref, b_ref, o_ref = refs
    d = x_ref.shape[1]
    pa = _dot(osb_ref[...], wpa_ref[...])
    pb = _dot(od_ref[...], wpb_ref[...])
    mixed = gate_ref[:, 0:d].astype(F32) * pa + gate_ref[:, d:2 * d].astype(F32) * pb
    mix = _dot(mixed.astype(BF16), wo_ref[...])
    x1 = _layer_norm(alpha * x_ref[...] + mix, g_ref[...], b_ref[...])
    o_ref[...] = x1
    if mode == "plain":
        return
    logits = _dot(x1.astype(BF16), wr_ref[...])
    lane_i = lax.broadcasted_iota(jnp.int32, logits.shape, 1)
    logits = jnp.where(lane_i < N_EXPERTS, logits, -jnp.inf)
    lane = lane_i.astype(F32)
    v1 = jnp.max(logits, axis=1, keepdims=True)
    i1 = jnp.min(jnp.where(logits == v1, lane, float(LANES)), axis=1, keepdims=True)
    rest = jnp.where(lane == i1, -jnp.inf, logits)
    v2 = jnp.max(rest, axis=1, keepdims=True)
    i2 = jnp.min(jnp.where(rest == v2, lane, float(LANES)), axis=1, keepdims=True)
    e = jnp.exp(v2 - v1)
    w1 = 1.0 / (1.0 + e)
    w2 = e / (1.0 + e)
    if mode == "gated":
        gates_ref[...] = jnp.where(lane == i1, w1, 0.0) + jnp.where(lane == i2, w2, 0.0)
        return
    _to_row_tiles(xt_ref, x1)

    @pl.when(pl.program_id(0) == 0)
    def _():
        cnt_s[...] = jnp.zeros_like(cnt_s)

    tm = logits.shape[0]
    sel = jnp.where(lane == i1, 1.0, 0.0) + jnp.where(lane == i2, 1.0, 0.0)
    earlier = (lax.broadcasted_iota(jnp.int32, (tm, tm), 1) < lax.broadcasted_iota(jnp.int32, (tm, tm), 0))
    before = _dot(jnp.where(earlier, 1.0, 0.0).astype(BF16), sel.astype(BF16)) + cnt_s[...]
    rank1 = jnp.sum(jnp.where(lane == i1, before, 0.0), axis=1, keepdims=True)
    rank2 = jnp.sum(jnp.where(lane == i2, before, 0.0), axis=1, keepdims=True)
    cnt_s[...] += jnp.sum(sel, axis=0, keepdims=True)
    route = jnp.zeros_like(logits)
    for idx, val in ((ROUTE_E1, i1), (ROUTE_E2, i2), (ROUTE_RANK1, rank1), (ROUTE_RANK2, rank2),
                     (ROUTE_W1, w1), (ROUTE_W2, w2)):
        route = jnp.where(lane_i == idx, val, route)
    route_ref[...] = route
    counts_ref[...] = jnp.broadcast_to(cnt_s[...], counts_ref.shape)


def _merge(o_sb, o_d, gate, x, w_pa, w_pb, w_o, ln_g, ln_b, layer, alpha, mode="plain", router_w=None,
           moe_index=None):
    n, d = x.shape
    tm = min(ROW_TILE, n)
    row = lambda i: (i, 0)
    wsel = lambda i: (layer, 0, 0)
    vsel = lambda i: (layer, 0, 0)
    in_specs = [pl.BlockSpec((tm, SB_WIDTH), row), pl.BlockSpec((tm, DIFF_V_WIDTH), row),
                pl.BlockSpec((tm, 2 * d), row), pl.BlockSpec((tm, d), row),
                pl.BlockSpec((None, SB_WIDTH, d), wsel), pl.BlockSpec((None, DIFF_V_WIDTH, d), wsel),
                pl.BlockSpec((None, d, d), wsel), pl.BlockSpec((None, 1, d), vsel), pl.BlockSpec((None, 1, d), vsel)]
    args = [o_sb, o_d, gate, x, w_pa, w_pb, w_o, ln_g, ln_b]
    out_shape = [jax.ShapeDtypeStruct((n, d), F32)]
    out_specs = [pl.BlockSpec((tm, d), row)]
    scratch = []
    if mode != "plain":
        in_specs.append(pl.BlockSpec((None, d, LANES), lambda i: (moe_index, 0, 0)))
        args.append(router_w)
    if mode == "gated":
        out_shape.append(jax.ShapeDtypeStruct((n, LANES), F32))
        out_specs.append(pl.BlockSpec((tm, LANES), row))
    if mode == "routed":
        out_shape += [jax.ShapeDtypeStruct((n, d // LANES, LANES), F32), jax.ShapeDtypeStruct((n, LANES), F32),
                      jax.ShapeDtypeStruct((SUBLANES, LANES), F32)]
        out_specs += [pl.BlockSpec((tm, d // LANES, LANES), lambda i: (i, 0, 0)), pl.BlockSpec((tm, LANES), row),
                      pl.BlockSpec((SUBLANES, LANES), lambda i: (0, 0))]
        scratch = [pltpu.VMEM((1, LANES), F32)]
    body = functools.partial(_merge_body, alpha=alpha, mode=mode)
    return pl.pallas_call(
        body, grid=(n // tm,), in_specs=in_specs, out_specs=out_specs, out_shape=out_shape,
        scratch_shapes=scratch, compiler_params=_params(("arbitrary",)), name=f"merge_{layer}_{n}")(*args)


def _swiglu_hidden(xb, wg, wu):
    hg = _dot(xb, wg)
    hu = _dot(xb, wu)
    return (hg * jax.nn.sigmoid(hg) * hu).astype(BF16)


def _ffn_body(x_ref, wg_ref, wu_ref, wd_ref, g_ref, b_ref, o_ref, acc_s, xb_s, *, alpha, n_ff):
    j = pl.program_id(1)

    @pl.when(j == 0)
    def _():
        acc_s[...] = jnp.zeros_like(acc_s)
        xb_s[...] = x_ref[...].astype(BF16)

    h = _swiglu_hidden(xb_s[...], wg_ref[...], wu_ref[...])
    acc_s[...] += _dot(h, wd_ref[...])

    @pl.when(j == n_ff - 1)
    def _():
        o_ref[...] = _layer_norm(alpha * x_ref[...] + acc_s[...], g_ref[...], b_ref[...])


def _ffn_dense(x, wg, wu, wd, ln_g, ln_b, layer, index, alpha):
    n, d = x.shape
    f = wg.shape[2]
    tm = min(FFN_ROW_TILE, n)
    tf = _largest_tile(f, 256)
    n_ff = f // tf
    body = functools.partial(_ffn_body, alpha=alpha, n_ff=n_ff)
    return pl.pallas_call(
        body, grid=(n // tm, n_ff),
        in_specs=[pl.BlockSpec((tm, d), lambda i, j: (i, 0)),
                  pl.BlockSpec((None, d, tf), lambda i, j: (index, 0, j)),
                  pl.BlockSpec((None, d, tf), lambda i, j: (index, 0, j)),
                  pl.BlockSpec((None, tf, d), lambda i, j: (index, j, 0)),
                  pl.BlockSpec((None, 1, d), lambda i, j: (layer, 0, 0)),
                  pl.BlockSpec((None, 1, d), lambda i, j: (layer, 0, 0))],
        out_specs=pl.BlockSpec((tm, d), lambda i, j: (i, 0)),
        out_shape=jax.ShapeDtypeStruct((n, d), F32),
        scratch_shapes=[pltpu.VMEM((tm, d), F32), pltpu.VMEM((tm, d), BF16)],
        compiler_params=_params(("arbitrary", "arbitrary")), name=f"ffn_{layer}_{n}")(x, wg, wu, wd, ln_g, ln_b)


def _moe_body(x_ref, gates_ref, wg_ref, wu_ref, wd_ref, g_ref, b_ref, o_ref, acc_s, xb_s, *, alpha, n_ff):
    e = pl.program_id(1)
    j = pl.program_id(2)

    @pl.when(jnp.logical_and(e == 0, j == 0))
    def _():
        acc_s[...] = jnp.zeros_like(acc_s)
        xb_s[...] = x_ref[...].astype(BF16)

    h = _swiglu_hidden(xb_s[...], wg_ref[...].astype(BF16), wu_ref[...].astype(BF16))
    y = _dot(h, wd_ref[...].astype(BF16))
    gates = gates_ref[...]
    lane = lax.broadcasted_iota(jnp.int32, gates.shape, 1)
    gate_e = jnp.sum(jnp.where(lane == e, gates, 0.0), axis=1, keepdims=True)
    acc_s[...] += gate_e * y

    @pl.when(jnp.logical_and(e == N_EXPERTS - 1, j == n_ff - 1))
    def _():
        o_ref[...] = _layer_norm(alpha * x_ref[...] + acc_s[...], g_ref[...], b_ref[...])


def _ffn_moe(x, gates, wg, wu, wd, ln_g, ln_b, layer, index, alpha):
    n, d = x.shape
    f = wg.shape[3]
    tm = min(FFN_ROW_TILE, n)
    tf = _largest_tile(f, 256)
    n_ff = f // tf
    body = functools.partial(_moe_body, alpha=alpha, n_ff=n_ff)
    return pl.pallas_call(
        body, grid=(n // tm, N_EXPERTS, n_ff),
        in_specs=[pl.BlockSpec((tm, d), lambda i, e, j: (i, 0)),
                  pl.BlockSpec((tm, LANES), lambda i, e, j: (i, 0)),
                  pl.BlockSpec((None, None, d, tf), lambda i, e, j: (index, e, 0, j)),
                  pl.BlockSpec((None, None, d, tf), lambda i, e, j: (index, e, 0, j)),
                  pl.BlockSpec((None, None, tf, d), lambda i, e, j: (index, e, j, 0)),
                  pl.BlockSpec((None, 1, d), lambda i, e, j: (layer, 0, 0)),
                  pl.BlockSpec((None, 1, d), lambda i, e, j: (layer, 0, 0))],
        out_specs=pl.BlockSpec((tm, d), lambda i, e, j: (i, 0)),
        out_shape=jax.ShapeDtypeStruct((n, d), F32),
        scratch_shapes=[pltpu.VMEM((tm, d), F32), pltpu.VMEM((tm, d), BF16)],
        compiler_params=_params(("arbitrary", "arbitrary", "arbitrary")),
        name=f"moe_{layer}_{n}")(x, gates, wg, wu, wd, ln_g, ln_b)


def _routing_tables(route, counts, row_tile, tok_tile):
    n = route.shape[0]
    as_int = lambda lane: route[:, lane].astype(jnp.int32)
    e1, e2, rank1, rank2 = (as_int(k) for k in (ROUTE_E1, ROUTE_E2, ROUTE_RANK1, ROUTE_RANK2))
    cnt = counts[0, :N_EXPERTS].astype(jnp.int32)
    padded = (cnt + row_tile - 1) // row_tile * row_tile
    ends = jnp.cumsum(padded)
    starts = ends - padded
    pos1 = (jnp.take(starts, e1) + rank1).reshape(n // tok_tile, 1, tok_tile)
    pos2 = (jnp.take(starts, e2) + rank2).reshape(n // tok_tile, 1, tok_tile)
    pos = jnp.concatenate([pos1, pos2], axis=2)
    n_row_tiles = TOP_K * n // row_tile + N_EXPERTS
    tile_start = jnp.arange(n_row_tiles, dtype=jnp.int32) * row_tile
    tile_expert = jnp.minimum(jnp.sum(tile_start[:, None] >= ends[None, :], axis=1), N_EXPERTS - 1)
    tile_used = (tile_start < ends[N_EXPERTS - 1]).astype(jnp.int32)
    return pos, tile_expert.astype(jnp.int32), tile_used, n_row_tiles


def _row_copy_wait(src_hbm, dst, sem):
    pltpu.make_async_copy(src_hbm.at[0], dst.at[0], sem).wait()


def _dispatch_body(pos_ref, xt_ref, rows_in, rows_hbm, sem, *, tok_tile):
    del rows_in

    def issue(t, carry):
        pltpu.make_async_copy(xt_ref.at[t], rows_hbm.at[pos_ref[0, t]], sem).start()
        pltpu.make_async_copy(xt_ref.at[t], rows_hbm.at[pos_ref[0, tok_tile + t]], sem).start()
        return carry

    lax.fori_loop(0, tok_tile, issue, 0)

    def drain(t, carry):
        _row_copy_wait(xt_ref, rows_hbm, sem)
        _row_copy_wait(xt_ref, rows_hbm, sem)
        return carry

    lax.fori_loop(0, tok_tile, drain, 0)


def _dispatch(xt, pos, n_rows, tok_tile):
    n = xt.shape[0]
    rows0 = jnp.zeros((n_rows,) + xt.shape[1:], F32)
    return pl.pallas_call(
        functools.partial(_dispatch_body, tok_tile=tok_tile), grid=(n // tok_tile,),
        in_specs=[pl.BlockSpec((None, 1, 2 * tok_tile), lambda i: (i, 0, 0), memory_space=pltpu.SMEM),
                  pl.BlockSpec((tok_tile,) + xt.shape[1:], lambda i: (i, 0, 0)),
                  pl.BlockSpec(memory_space=pl.ANY)],
        out_specs=pl.BlockSpec(memory_space=pl.ANY),
        out_shape=jax.ShapeDtypeStruct(rows0.shape, F32),
        scratch_shapes=[pltpu.SemaphoreType.DMA(())],
        input_output_aliases={2: 0},
        compiler_params=_params(("arbitrary",)), name="moe_dispatch")(pos, xt, rows0)


def _expert_rows_body(te_ref, tu_ref, x_ref, wg_ref, wu_ref, wd_ref, y_ref, acc_s, xb_s, *, n_ff):
    del te_ref
    t = pl.program_id(0)
    j = pl.program_id(1)
    used = tu_ref[t] == 1

    @pl.when(j == 0)
    def _():
        acc_s[...] = jnp.zeros_like(acc_s)
        xb_s[...] = _from_row_tiles(x_ref).astype(BF16)

    @pl.when(used)
    def _():
        h = _swiglu_hidden(xb_s[...], wg_ref[...].astype(BF16), wu_ref[...].astype(BF16))
        acc_s[...] += _dot(h, wd_ref[...].astype(BF16))

    @pl.when(j == n_ff - 1)
    def _():
        _to_row_tiles(y_ref, acc_s[...])


def _expert_rows(rows, tile_expert, tile_used, wg, wu, wd, index, row_tile, n_row_tiles):
    d = rows.shape[1] * rows.shape[2]
    f = wg.shape[3]
    tf = _largest_tile(f, 256)
    n_ff = f // tf
    wcol = lambda t, j, te, tu: (index, te[t], 0, j * tu[t])
    wrow = lambda t, j, te, tu: (index, te[t], j * tu[t], 0)
    rmap = lambda t, j, te, tu: (t, 0, 0)
    grid_spec = pltpu.PrefetchScalarGridSpec(
        num_scalar_prefetch=2, grid=(n_row_tiles, n_ff),
        in_specs=[pl.BlockSpec((row_tile,) + rows.shape[1:], rmap),
                  pl.BlockSpec((None, None, d, tf), wcol), pl.BlockSpec((None, None, d, tf), wcol),
                  pl.BlockSpec((None, None, tf, d), wrow)],
        out_specs=pl.BlockSpec((row_tile,) + rows.shape[1:], rmap),
        scratch_shapes=[pltpu.VMEM((row_tile, d), F32), pltpu.VMEM((row_tile, d), BF16)])
    return pl.pallas_call(
        functools.partial(_expert_rows_body, n_ff=n_ff), grid_spec=grid_spec,
        out_shape=jax.ShapeDtypeStruct(rows.shape, F32),
        compiler_params=_params(("arbitrary", "arbitrary")), name="moe_experts")(
            tile_expert, tile_used, rows, wg, wu, wd)


def _combine_body(pos_ref, y_hbm, x_ref, route_ref, g_ref, b_ref, o_ref, buf, sem, *, tok_tile, alpha):
    def issue(t, carry):
        pltpu.make_async_copy(y_hbm.at[pos_ref[0, t]], buf.at[0, t], sem).start()
        pltpu.make_async_copy(y_hbm.at[pos_ref[0, tok_tile + t]], buf.at[1, t], sem).start()
        return carry

    lax.fori_loop(0, tok_tile, issue, 0)

    def drain(t, carry):
        _row_copy_wait(y_hbm, buf.at[0], sem)
        _row_copy_wait(y_hbm, buf.at[1], sem)
        return carry

    lax.fori_loop(0, tok_tile, drain, 0)
    route = route_ref[...]
    lane = lax.broadcasted_iota(jnp.int32, route.shape, 1)
    w1 = jnp.sum(jnp.where(lane == ROUTE_W1, route, 0.0), axis=1, keepdims=True)
    w2 = jnp.sum(jnp.where(lane == ROUTE_W2, route, 0.0), axis=1, keepdims=True)
    y = w1 * _from_row_tiles(buf.at[0]) + w2 * _from_row_tiles(buf.at[1])
    o_ref[...] = _layer_norm(alpha * x_ref[...] + y, g_ref[...], b_ref[...])


def _combine(y_rows, pos, x, route, ln_g, ln_b, layer, alpha, tok_tile):
    n, d = x.shape
    row = lambda i: (i, 0)
    return pl.pallas_call(
        functools.partial(_combine_body, tok_tile=tok_tile, alpha=alpha), grid=(n // tok_tile,),
        in_specs=[pl.BlockSpec((None, 1, 2 * tok_tile), lambda i: (i, 0, 0), memory_space=pltpu.SMEM),
                  pl.BlockSpec(memory_space=pl.ANY),
                  pl.BlockSpec((tok_tile, d), row), pl.BlockSpec((tok_tile, LANES), row),
                  pl.BlockSpec((None, 1, d), lambda i: (layer, 0, 0)),
                  pl.BlockSpec((None, 1, d), lambda i: (layer, 0, 0))],
        out_specs=pl.BlockSpec((tok_tile, d), row),
        out_shape=jax.ShapeDtypeStruct((n, d), F32),
        scratch_shapes=[pltpu.VMEM((TOP_K, tok_tile) + y_rows.shape[1:], F32), pltpu.SemaphoreType.DMA(())],
        compiler_params=_params(("arbitrary",)), name=f"moe_combine_{layer}")(pos, y_rows, x, route, ln_g, ln_b)


def _ffn_moe_routed(x, xt, route, counts, wg, wu, wd, ln_g, ln_b, layer, index, alpha):
    n = x.shape[0]
    row_tile = min(MOE_ROW_TILE, n)
    tok_tile = min(MOE_TOKEN_TILE, n)
    pos, tile_expert, tile_used, n_row_tiles = _routing_tables(route, counts, row_tile, tok_tile)
    rows = _dispatch(xt, pos, n_row_tiles * row_tile, tok_tile)
    y_rows = _expert_rows(rows, tile_expert, tile_used, wg, wu, wd, index, row_tile, n_row_tiles)
    return _combine(y_rows, pos, x, route, ln_g, ln_b, layer, alpha, tok_tile)


def _prepare_in_proj(w_in):
    depth, d, _ = w_in.shape

    def pair(a, b):
        a = a.reshape(depth, d, DIFF_HEADS, 64)
        b = b.reshape(depth, d, DIFF_HEADS, 64)
        return jnp.concatenate([a, b], axis=-1).reshape(depth, d, 2 * DIFF_QK_WIDTH)

    c = [0, 512, 1024, 1536, 1792, 2048, 2304, 2560, 3072, 3072 + 2 * d]
    part = [w_in[:, :, c[k]:c[k + 1]] for k in range(9)]
    q_sb, k_sb, v_sb, q1, q2, k1, k2, v_d, gate = part
    w_main = jnp.concatenate([q_sb, pair(q1, q2), pair(k1, k2), v_d, gate], axis=-1).astype(BF16)
    w_kvt = jnp.swapaxes(jnp.concatenate([k_sb, v_sb], axis=-1), 1, 2).astype(BF16)
    return w_main, w_kvt


def kernel(x_prompt, x_sample, cache_sb_k, cache_sb_v, cache_diff_k, cache_diff_v, page_table, rel_bias, w_in,
           w_pa, w_pb, w_o, diff_lambda, diff_subln_g, ln1_g, ln1_b, ln2_g, ln2_b, ffn_w_gate, ffn_w_up,
           ffn_w_down, router_w, moe_w_gate, moe_w_up, moe_w_down):
    batch, seq, d = x_prompt.shape
    db = x_sample.shape[0]
    depth = w_in.shape[0]
    n_pool = cache_sb_k.shape[1]
    alpha = (2.0 * depth) ** 0.25
    n = batch * seq

    w_main, w_kvt = _prepare_in_proj(w_in)
    w_pa_b, w_pb_b, w_o_b = w_pa.astype(BF16), w_pb.astype(BF16), w_o.astype(BF16)
    ffn_g_b, ffn_u_b, ffn_d_b = ffn_w_gate.astype(BF16), ffn_w_up.astype(BF16), ffn_w_down.astype(BF16)
    router_b = jnp.pad(router_w, ((0, 0), (0, 0), (0, LANES - N_EXPERTS))).astype(BF16)

    sbk_t = jnp.transpose(cache_sb_k, (0, 1, 3, 4, 2)).reshape(depth, n_pool, SB_WIDTH, PAGE_SIZE)
    sbv_t = jnp.transpose(cache_sb_v, (0, 1, 3, 4, 2)).reshape(depth, n_pool, SB_WIDTH, PAGE_SIZE)
    dk_r = cache_diff_k.reshape(depth, n_pool, PAGE_SIZE * DIFF_HEADS, LANES)
    dv_r = cache_diff_v.reshape(depth, n_pool, PAGE_SIZE * DIFF_HEADS, LANES)

    ln1_g, ln1_b, ln2_g, ln2_b = (v.reshape(depth, 1, d) for v in (ln1_g, ln1_b, ln2_g, ln2_b))
    diff_subln_g = diff_subln_g.reshape(depth, 1, DIFF_V_DIM)

    bias_t = _bias_tiles(rel_bias, min(DIFF_TILE, seq))
    bias_s = _bias_sample(rel_bias)

    xp = x_prompt.reshape(n, d)
    xs = x_sample.reshape(db, d)
    stacked = None
    s_ksb, s_vsb, s_k12, s_vd = [], [], [], []
    for l in range(depth):
        lam_init = 0.8 - 0.6 * math.exp(-0.3 * l)
        is_moe = l % 2 == 1
        idx = l // 2

        q_sb, q12, gate, k12_all, vd_all, kt_all, vt_all = _proj_prompt(xp, w_main, w_kvt, l, stacked, batch, seq)
        stacked = (k12_all, vd_all, kt_all, vt_all)
        o_sb = _sb_prompt(q_sb, kt_all, vt_all, l, batch, seq)
        o_d = _diff_prompt(q12, k12_all, vd_all, bias_t, diff_lambda, diff_subln_g, l, batch, seq, lam_init)
        if is_moe:
            xp, xt, route, counts = _merge(o_sb, o_d, gate, xp, w_pa_b, w_pb_b, w_o_b, ln1_g, ln1_b, l, alpha,
                                           "routed", router_b, idx)
            xp = _ffn_moe_routed(xp, xt, route, counts, moe_w_gate, moe_w_up, moe_w_down, ln2_g, ln2_b, l, idx,
                                 alpha)
        else:
            (xp,) = _merge(o_sb, o_d, gate, xp, w_pa_b, w_pb_b, w_o_b, ln1_g, ln1_b, l, alpha)
            xp = _ffn_dense(xp, ffn_g_b, ffn_u_b, ffn_d_b, ln2_g, ln2_b, l, idx, alpha)

        sq_sb, sq12, sgate, sk12, svd, sksb, svsb = _proj_sample(xs, w_main, w_kvt, l)
        so_sb = _sb_sample(sq_sb, sbk_t, sbv_t, page_table, l)
        so_d = _diff_sample(sq12, sk12, svd, dk_r, dv_r, page_table, bias_s, diff_lambda, diff_subln_g, l, lam_init)
        if is_moe:
            xs, sgates = _merge(so_sb, so_d, sgate, xs, w_pa_b, w_pb_b, w_o_b, ln1_g, ln1_b, l, alpha,
                                "gated", router_b, idx)
            xs = _ffn_moe(xs, sgates, moe_w_gate, moe_w_up, moe_w_down, ln2_g, ln2_b, l, idx, alpha)
        else:
            (xs,) = _merge(so_sb, so_d, sgate, xs, w_pa_b, w_pb_b, w_o_b, ln1_g, ln1_b, l, alpha)
            xs = _ffn_dense(xs, ffn_g_b, ffn_u_b, ffn_d_b, ln2_g, ln2_b, l, idx, alpha)
        s_ksb.append(sksb)
        s_vsb.append(svsb)
        s_k12.append(sk12)
        s_vd.append(svd)

    k12_all, vd_all, kt_all, vt_all = stacked
    to_sb = lambda t: jnp.transpose(t.reshape(depth, batch, SB_HEADS, 64, seq), (0, 1, 4, 2, 3))
    to_diff = lambda t: t.reshape(depth, batch, seq, DIFF_HEADS, DIFF_V_DIM)
    return (xp.reshape(batch, seq, d), xs.reshape(db, 1, d),
            to_sb(kt_all), to_sb(vt_all), to_diff(k12_all), to_diff(vd_all),
            jnp.stack(s_ksb).reshape(depth, db, 1, SB_HEADS, 64),
            jnp.stack(s_vsb).reshape(depth, db, 1, SB_HEADS, 64),
            jnp.stack(s_k12).reshape(depth, db, 1, DIFF_HEADS, DIFF_V_DIM),
            jnp.stack(s_vd).reshape(depth, db, 1, DIFF_HEADS, DIFF_V_DIM))
```

```python
import functools
import math

import jax
import jax.numpy as jnp
from jax import lax
from jax.experimental import pallas as pl
from jax.experimental.pallas import tpu as pltpu

F32 = jnp.float32
BF16 = jnp.bfloat16

PAGE_SIZE = 128
SB_HEADS = 8
SB_WIDTH = 512
DIFF_HEADS = 4
DIFF_V_DIM = 128
DIFF_QK_WIDTH = 256
DIFF_V_WIDTH = 512
N_EXPERTS = 8
TOP_K = 2
N_BUCKETS = 32
MAX_DISTANCE = 128
LN_EPS = 1e-5
SUBLN_EPS = 1e-5
NEG_INF = -1e30
QK_SCALE = 0.125

LANES = 128
SUBLANES = 8
VMEM_LIMIT_BYTES = 56 * 1024 * 1024

ROW_TILE = 512
FFN_ROW_TILE = 1024
DIFF_TILE = 512
SB_TILE_Q = 512
SB_TILE_K = 512
SB_CUM_WIDTH = 256
DEC_PAGES = 8
SB_HEADS_PER_STEP = 4
DIFF_HEADS_PER_STEP = 2
MOE_ROW_TILE = 1024
MOE_TOKEN_TILE = 512

NT_DIMS = (((1,), (1,)), ((), ()))


def _params(semantics):
    return pltpu.CompilerParams(dimension_semantics=semantics, vmem_limit_bytes=VMEM_LIMIT_BYTES)


def _dot(a, b):
    return jnp.dot(a, b, preferred_element_type=F32)


def _dot_nt(a, b):
    return lax.dot_general(a, b, NT_DIMS, preferred_element_type=F32)


def _layer_norm(y, g, b):
    mu = jnp.mean(y, axis=-1, keepdims=True)
    d = y - mu
    var = jnp.mean(d * d, axis=-1, keepdims=True)
    return d * lax.rsqrt(var + LN_EPS) * g + b


def _strict_lower_ones(n):
    j = lax.broadcasted_iota(jnp.int32, (n, n), 0)
    s = lax.broadcasted_iota(jnp.int32, (n, n), 1)
    return jnp.where(j > s, 1.0, 0.0).astype(BF16)


def _largest_tile(dim, cap):
    if dim <= cap:
        return dim
    best = None
    for t in range(LANES, cap + 1, LANES):
        if dim % t == 0:
            best = t
    assert best is not None, (dim, cap)
    return best


def _store_head_rows(ref, x):
    rows = x.shape[0]
    for hd in range(DIFF_HEADS):
        ref[pl.ds(hd, rows, stride=DIFF_HEADS), :] = x[:, hd * LANES:(hd + 1) * LANES]


def _proj_prompt_body(x_ref, wm_ref, wkvt_ref, *rest):
    qsb_ref, q12_ref, gate_ref, k12_ref, vd_ref, ksbt_ref, vsbt_ref = rest[-7:]
    x = x_ref[...].astype(BF16)

    def cols(c0, n):
        return _dot(x, wm_ref[:, c0:c0 + n])

    qsb_ref[...] = (cols(0, 512) * QK_SCALE).astype(BF16)
    q12_ref[...] = (cols(512, 512) * QK_SCALE).astype(BF16)
    _store_head_rows(k12_ref, cols(1024, 512))
    _store_head_rows(vd_ref, cols(1536, 512))
    gate_ref[...] = jax.nn.sigmoid(cols(2048, 2048)).astype(BF16)
    ksbt_ref[...] = _dot_nt(wkvt_ref[0:512, :], x)
    vsbt_ref[...] = _dot_nt(wkvt_ref[512:1024, :], x)


def _proj_prompt(x, w_main, w_kvt, layer, stacked, batch, seq):
    n, d = x.shape
    depth = w_main.shape[0]
    tm = min(ROW_TILE, seq)
    nt = seq // tm
    grid = (n // tm,)
    row = lambda i: (i, 0)
    in_specs = [
        pl.BlockSpec((tm, d), row),
        pl.BlockSpec((None, d, w_main.shape[2]), lambda i: (layer, 0, 0)),
        pl.BlockSpec((None, w_kvt.shape[1], d), lambda i: (layer, 0, 0)),
    ]
    args = [x, w_main, w_kvt]
    aliases = {}
    if stacked is not None:
        in_specs += [pl.BlockSpec(memory_space=pl.ANY)] * 4
        args += list(stacked)
        aliases = {3: 3, 4: 4, 5: 5, 6: 6}
    out_shape = [
        jax.ShapeDtypeStruct((n, 512), BF16),
        jax.ShapeDtypeStruct((n, 512), BF16),
        jax.ShapeDtypeStruct((n, 2048), BF16),
        jax.ShapeDtypeStruct((depth, DIFF_HEADS * n, LANES), F32),
        jax.ShapeDtypeStruct((depth, DIFF_HEADS * n, LANES), F32),
        jax.ShapeDtypeStruct((depth, batch, 512, seq), F32),
        jax.ShapeDtypeStruct((depth, batch, 512, seq), F32),
    ]
    out_specs = [
        pl.BlockSpec((tm, 512), row),
        pl.BlockSpec((tm, 512), row),
        pl.BlockSpec((tm, 2048), row),
        pl.BlockSpec((None, DIFF_HEADS * tm, LANES), lambda i: (layer, i, 0)),
        pl.BlockSpec((None, DIFF_HEADS * tm, LANES), lambda i: (layer, i, 0)),
        pl.BlockSpec((None, None, 512, tm), lambda i: (layer, i // nt, 0, i % nt)),
        pl.BlockSpec((None, None, 512, tm), lambda i: (layer, i // nt, 0, i % nt)),
    ]
    return pl.pallas_call(
        _proj_prompt_body, grid=grid, in_specs=in_specs, out_specs=out_specs, out_shape=out_shape,
        input_output_aliases=aliases, compiler_params=_params(("arbitrary",)),
        name=f"proj_prompt_{layer}")(*args)


def _proj_sample_body(x_ref, wm_ref, wkvt_ref, qsb_ref, q12_ref, gate_ref, k12_ref, vd_ref, ksb_ref, vsb_ref):
    x = x_ref[...].astype(BF16)

    def cols(c0, n):
        return _dot(x, wm_ref[:, c0:c0 + n])

    qsb_ref[...] = (cols(0, 512) * QK_SCALE).astype(BF16)
    q12_ref[...] = (cols(512, 512) * QK_SCALE).astype(BF16)
    k12_ref[...] = cols(1024, 512)
    vd_ref[...] = cols(1536, 512)
    gate_ref[...] = jax.nn.sigmoid(cols(2048, 2048)).astype(BF16)
    ksb_ref[...] = _dot_nt(x, wkvt_ref[0:512, :])
    vsb_ref[...] = _dot_nt(x, wkvt_ref[512:1024, :])


def _proj_sample(x, w_main, w_kvt, layer):
    n, d = x.shape
    full = lambda shape: pl.BlockSpec(shape, lambda i: (0,) * len(shape))
    out_shape = [
        jax.ShapeDtypeStruct((n, 512), BF16), jax.ShapeDtypeStruct((n, 512), BF16),
        jax.ShapeDtypeStruct((n, 2048), BF16),
        jax.ShapeDtypeStruct((n, 512), F32), jax.ShapeDtypeStruct((n, 512), F32),
        jax.ShapeDtypeStruct((n, 512), F32), jax.ShapeDtypeStruct((n, 512), F32),
    ]
    return pl.pallas_call(
        _proj_sample_body, grid=(1,),
        in_specs=[full((n, d)),
                  pl.BlockSpec((None, d, w_main.shape[2]), lambda i: (layer, 0, 0)),
                  pl.BlockSpec((None, w_kvt.shape[1], d), lambda i: (layer, 0, 0))],
        out_specs=[full(s.shape) for s in out_shape], out_shape=out_shape,
        compiler_params=_params(("arbitrary",)), name=f"proj_sample_{layer}")(x, w_main, w_kvt)


def _softplus(z):
    return jnp.maximum(z, 0.0) + jnp.log(1.0 + jnp.exp(-jnp.abs(z)))


def _neg_suffix_sums(sp, neg_tri):
    w = neg_tri.shape[0]
    parts = []
    carry = None
    for c in reversed(range(sp.shape[1] // w)):
        blk = sp[:, c * w:(c + 1) * w]
        hi = blk.astype(BF16)
        lo = (blk - hi.astype(F32)).astype(BF16)
        cum = _dot(hi, neg_tri) + _dot(lo, neg_tri)
        if carry is not None:
            cum = cum + carry
        parts.insert(0, cum)
        carry = cum[:, 0:1] - blk[:, 0:1]
    return parts[0] if len(parts) == 1 else jnp.concatenate(parts, axis=1)


def _sb_block(q, k_t, v_t, neg_tri, r, acc, causal):
    z = _dot(q, k_t)
    sp = _softplus(z)
    log_beta = z - sp
    if causal is not None:
        sp = jnp.where(causal, sp, 0.0)
    cum = _neg_suffix_sums(sp, neg_tri)
    w = jnp.exp(log_beta + cum + r)
    if causal is not None:
        w = jnp.where(causal, w, 0.0)
    acc = acc + _dot_nt(w.astype(BF16), v_t)
    r = r + cum[:, 0:1] - sp[:, 0:1]
    return r, acc


def _sb_prompt_body(q_ref, kt_ref, vt_ref, o_ref, kt_s, vt_s, r_s, acc_s, *, tq, tk, n_blocks, n_pairs):
    i = pl.program_id(2)
    ratio = tq // tk

    @pl.when(i == 0)
    def _():
        for c in range(n_blocks):
            kt_s[c] = kt_ref[:, c * tk:(c + 1) * tk].astype(BF16)
            vt_s[c] = vt_ref[:, c * tk:(c + 1) * tk].astype(BF16)

    lane = lax.broadcasted_iota(jnp.int32, (tq, LANES), 1)
    q_heads = []
    for hp in range(n_pairs):
        q2 = q_ref[:, hp * LANES:(hp + 1) * LANES]
        zero = jnp.zeros_like(q2)
        q_heads += [jnp.where(lane < 64, q2, zero), jnp.where(lane >= 64, q2, zero)]
    row = lax.broadcasted_iota(jnp.int32, (tq, tk), 0)
    col = lax.broadcasted_iota(jnp.int32, (tq, tk), 1)
    neg_tri = -_strict_lower_ones(min(SB_CUM_WIDTH, tk))
    r_s[...] = jnp.zeros_like(r_s)
    acc_s[...] = jnp.zeros_like(acc_s)

    def step(kb, mask):
        for h in range(2 * n_pairs):
            rows = slice((h // 2) * LANES, (h // 2 + 1) * LANES)
            r, acc = _sb_block(q_heads[h], kt_s[kb, rows, :], vt_s[kb, rows, :], neg_tri, r_s[h], acc_s[h], mask)
            r_s[h] = r
            acc_s[h] = acc

    for d in reversed(range(ratio)):
        step(i * ratio + d, col + d * tk < row)

    def body(j, carry):
        step(i * ratio - 1 - j, None)
        return carry

    lax.fori_loop(0, i * ratio, body, 0)
    for hp in range(n_pairs):
        o_ref[:, hp * LANES:(hp + 1) * LANES] = jnp.where(lane < 64, acc_s[2 * hp], acc_s[2 * hp + 1]).astype(BF16)


def _sb_prompt(q, kt_all, vt_all, layer, batch, seq):
    tq = min(SB_TILE_Q, seq)
    tk = min(SB_TILE_K, tq)
    nq = seq // tq
    n_pairs = SB_HEADS_PER_STEP // 2
    width = n_pairs * LANES
    body = functools.partial(_sb_prompt_body, tq=tq, tk=tk, n_blocks=seq // tk, n_pairs=n_pairs)
    qmap = lambda b, h, i: (b * nq + i, h)
    kvmap = lambda b, h, i: (layer, b, h, 0)
    return pl.pallas_call(
        body, grid=(batch, SB_HEADS // SB_HEADS_PER_STEP, nq),
        in_specs=[pl.BlockSpec((tq, width), qmap),
                  pl.BlockSpec((None, None, width, seq), kvmap),
                  pl.BlockSpec((None, None, width, seq), kvmap)],
        out_specs=pl.BlockSpec((tq, width), qmap),
        out_shape=jax.ShapeDtypeStruct((batch * seq, SB_WIDTH), BF16),
        scratch_shapes=[pltpu.VMEM((seq // tk, width, tk), BF16), pltpu.VMEM((seq // tk, width, tk), BF16),
                        pltpu.VMEM((SB_HEADS_PER_STEP, tq, 1), F32),
                        pltpu.VMEM((SB_HEADS_PER_STEP, tq, LANES), F32)],
        compiler_params=_params(("arbitrary", "arbitrary", "arbitrary")),
        name=f"sb_prompt_{layer}")(q, kt_all, vt_all)


def _sb_sample_body(pt_ref, q_ref, *rest, n_pages_step, n_steps):
    del pt_ref
    k_refs = rest[:n_pages_step]
    v_refs = rest[n_pages_step:2 * n_pages_step]
    o_ref, kt_s, vt_s, r_s, acc_s = rest[2 * n_pages_step:]
    g = pl.program_id(1)
    page = lambda p: slice(p * PAGE_SIZE, (p + 1) * PAGE_SIZE)

    @pl.when(g == 0)
    def _():
        r_s[...] = jnp.zeros_like(r_s)
        acc_s[...] = jnp.zeros_like(acc_s)

    for p in range(n_pages_step):
        kt_s[:, page(p)] = k_refs[p][...].astype(BF16)
        vt_s[:, page(p)] = v_refs[p][...].astype(BF16)

    q = q_ref[...]
    row = lax.broadcasted_iota(jnp.int32, (SB_HEADS, SB_WIDTH), 0)
    col = lax.broadcasted_iota(jnp.int32, (SB_HEADS, SB_WIDTH), 1)
    own = (col >> 6) == row
    q_bd = jnp.where(own, jnp.broadcast_to(q.astype(F32), (SB_HEADS, SB_WIDTH)), 0.0).astype(BF16)
    z = _dot(q_bd, kt_s[...])
    sp = _softplus(z)
    log_beta = z - sp
    sp_rows = jnp.concatenate([sp[:, page(p)] for p in range(n_pages_step)], axis=0)
    cum_rows = _neg_suffix_sums(sp_rows, -_strict_lower_ones(PAGE_SIZE))
    r = r_s[...]
    w = [None] * n_pages_step
    for p in reversed(range(n_pages_step)):
        cum = cum_rows[p * SB_HEADS:(p + 1) * SB_HEADS]
        sp_p = sp[:, page(p)]
        w[p] = jnp.exp(log_beta[:, page(p)] + cum + r)
        r = r + jnp.broadcast_to(cum[:, 0:1] - sp_p[:, 0:1], r.shape)
    r_s[...] = r
    acc = acc_s[...] + _dot_nt(jnp.concatenate(w, axis=1).astype(BF16), vt_s[...])
    acc_s[...] = acc

    @pl.when(g == n_steps - 1)
    def _():
        o_ref[...] = jnp.sum(jnp.where(own, acc, 0.0), axis=0, keepdims=True).astype(BF16)


def _sb_sample(q, cache_kt, cache_vt, page_table, layer):
    db = q.shape[0]
    n_pages = page_table.shape[1]
    pps = min(DEC_PAGES, n_pages)
    n_steps = n_pages // pps
    q3 = q.reshape(db, 1, SB_WIDTH)

    def page_spec(p):
        return pl.BlockSpec((None, None, SB_WIDTH, PAGE_SIZE),
                            lambda b, g, pt: (layer, pt[b, (n_steps - 1 - g) * pps + p], 0, 0))

    qspec = pl.BlockSpec((None, 1, SB_WIDTH), lambda b, g, pt: (b, 0, 0))
    grid_spec = pltpu.PrefetchScalarGridSpec(
        num_scalar_prefetch=1, grid=(db, n_steps),
        in_specs=[qspec] + [page_spec(p) for p in range(pps)] * 2,
        out_specs=qspec,
        scratch_shapes=[pltpu.VMEM((SB_WIDTH, pps * PAGE_SIZE), BF16), pltpu.VMEM((SB_WIDTH, pps * PAGE_SIZE), BF16),
                        pltpu.VMEM((SB_HEADS, LANES), F32), pltpu.VMEM((SB_HEADS, SB_WIDTH), F32)])
    body = functools.partial(_sb_sample_body, n_pages_step=pps, n_steps=n_steps)
    out = pl.pallas_call(
        body, grid_spec=grid_spec, out_shape=jax.ShapeDtypeStruct((db, 1, SB_WIDTH), BF16),
        compiler_params=_params(("arbitrary", "arbitrary")), name=f"sb_sample_{layer}")(
            page_table, q3, *([cache_kt] * pps), *([cache_vt] * pps))
    return out.reshape(db, SB_WIDTH)


def _t5_bucket(n):
    max_exact = N_BUCKETS // 2
    nf = jnp.maximum(n, 1).astype(F32)
    large = max_exact + (jnp.log(nf / max_exact) / math.log(MAX_DISTANCE / max_exact)
                         * (N_BUCKETS - max_exact)).astype(jnp.int32)
    large = jnp.minimum(large, N_BUCKETS - 1)
    return jnp.where(n < max_exact, n, large)


def _bias_lookup(bucket, head, rb_ref, n_heads_static):
    val = jnp.zeros(bucket.shape, F32)
    for k in range(N_BUCKETS):
        if isinstance(head, int):
            val = jnp.where(bucket == k, rb_ref[k, head], val)
        else:
            for hh in range(n_heads_static):
                val = jnp.where((bucket == k) & (head == hh), rb_ref[k, hh], val)
    return val


def _bias_tiles_body(rb_ref, o_ref, *, tile):
    h = pl.program_id(0)
    which = pl.program_id(1)
    row = lax.broadcasted_iota(jnp.int32, (tile, tile), 0)
    col = lax.broadcasted_iota(jnp.int32, (tile, tile), 1)
    n = jnp.maximum(row - col + which * tile, 0)
    bucket = _t5_bucket(n)
    val = jnp.zeros((tile, tile), F32)
    for k in range(N_BUCKETS):
        val = jnp.where(bucket == k, rb_ref[k, h], val)
    o_ref[...] = val - rb_ref[N_BUCKETS - 1, h]


def _bias_tiles(rel_bias, tile):
    return pl.pallas_call(
        functools.partial(_bias_tiles_body, tile=tile), grid=(DIFF_HEADS, 2),
        in_specs=[pl.BlockSpec(memory_space=pltpu.SMEM)],
        out_specs=pl.BlockSpec((None, None, tile, tile), lambda h, w: (h, w, 0, 0)),
        out_shape=jax.ShapeDtypeStruct((DIFF_HEADS, 2, tile, tile), F32),
        compiler_params=_params(("arbitrary", "arbitrary")), name="bias_tiles")(rel_bias)


def _bias_sample_body(rb_ref, o_ref):
    shape = (2 * DIFF_HEADS, DIFF_HEADS * PAGE_SIZE)
    head = lax.broadcasted_iota(jnp.int32, shape, 0) & (DIFF_HEADS - 1)
    pos = lax.broadcasted_iota(jnp.int32, shape, 1) >> 2
    far = jnp.zeros(shape, jnp.int32) + MAX_DISTANCE
    last = PAGE_SIZE - pos
    new = jnp.zeros(shape, jnp.int32)
    far_bias = _bias_lookup(_t5_bucket(far), head, rb_ref, DIFF_HEADS)
    for idx, n in enumerate((last, new)):
        o_ref[idx] = _bias_lookup(_t5_bucket(n), head, rb_ref, DIFF_HEADS) - far_bias


def _bias_sample(rel_bias):
    shape = (2, 2 * DIFF_HEADS, DIFF_HEADS * PAGE_SIZE)
    return pl.pallas_call(
        _bias_sample_body, grid=(1,),
        in_specs=[pl.BlockSpec(memory_space=pltpu.SMEM)],
        out_specs=pl.BlockSpec(shape, lambda i: (0, 0, 0)),
        out_shape=jax.ShapeDtypeStruct(shape, F32),
        compiler_params=_params(("arbitrary",)), name="bias_sample")(rel_bias)


def _diff_lambda(lp, lam_init):
    a = jnp.sum(lp[0:1] * lp[1:2], axis=1, keepdims=True)
    b = jnp.sum(lp[2:3] * lp[3:4], axis=1, keepdims=True)
    return jnp.exp(a) - jnp.exp(b) + lam_init


def _softmax_step(s, v, m, l, acc):
    m_new = jnp.maximum(m, jnp.max(s, axis=1, keepdims=True))
    a = jnp.exp(m - m_new)
    p = jnp.exp(s - m_new)
    l = a * l + jnp.sum(p, axis=1, keepdims=True)
    acc = a * acc + _dot(p.astype(BF16), v)
    return m_new, l, acc


def _sub_layer_norm(o, g, lam_init):
    o = o * lax.rsqrt(jnp.mean(o * o, axis=-1, keepdims=True) + SUBLN_EPS)
    return o * g * (1.0 - lam_init)


def _diff_prompt_body(lam_ref, g_ref, q_ref, k_ref, v_ref, d_ref, o_ref,
                      k_s, v_s, m_s, acc_s, *, tile, lam_init, n_heads):
    i = pl.program_id(2)
    head = lambda hd: slice(hd * LANES, (hd + 1) * LANES)
    seq = k_s.shape[0]

    @pl.when(i == 0)
    def _():
        for hd in range(n_heads):
            rows = pl.ds(pl.program_id(1) * n_heads + hd, seq, stride=DIFF_HEADS)
            k_s[:, head(hd)] = k_ref[rows, :].astype(BF16)
            v_s[hd, :, 0:LANES] = v_ref[rows, :].astype(BF16)
            v_s[hd, :, LANES:2 * LANES] = jnp.ones((seq, LANES), BF16)

    lane = lax.broadcasted_iota(jnp.int32, (tile, LANES), 1)
    q_maps = []
    for hd in range(n_heads):
        q2 = q_ref[:, head(hd)]
        zero = jnp.zeros_like(q2)
        q_maps += [jnp.where(lane < 64, q2, zero), jnp.where(lane >= 64, q2, zero)]
    row = lax.broadcasted_iota(jnp.int32, (tile, tile), 0)
    col = lax.broadcasted_iota(jnp.int32, (tile, tile), 1)
    visible = col <= row
    m_s[...] = jnp.full_like(m_s, NEG_INF)
    acc_s[...] = jnp.zeros_like(acc_s)

    def step(kb, near, mask):
        start = pl.multiple_of(kb * tile, tile)
        for c in range(2 * n_heads):
            hd = c // 2
            s = _dot_nt(q_maps[c], k_s[pl.ds(start, tile), head(hd)])
            if near is not None:
                s = s + d_ref[hd, near]
            if mask is not None:
                s = jnp.where(mask, s, NEG_INF)
            m_old = m_s[c]
            m_new = jnp.maximum(m_old, jnp.max(s, axis=1, keepdims=True))
            p = jnp.exp(s - m_new)
            acc_s[c] = jnp.exp(m_old - m_new) * acc_s[c] + _dot(p.astype(BF16), v_s[hd, pl.ds(start, tile), :])
            m_s[c] = m_new

    step(i, 0, visible)

    @pl.when(i >= 1)
    def _():
        step(i - 1, 1, None)

    def body(j, carry):
        step(j, None, None)
        return carry

    lax.fori_loop(0, i - 1, body, 0)
    lam = _diff_lambda(lam_ref[...], lam_init)
    for hd in range(n_heads):
        a1, a2 = acc_s[2 * hd], acc_s[2 * hd + 1]
        o = a1[:, 0:LANES] / a1[:, LANES:2 * LANES] - lam * (a2[:, 0:LANES] / a2[:, LANES:2 * LANES])
        o_ref[:, head(hd)] = _sub_layer_norm(o, g_ref[...], lam_init).astype(BF16)


def _diff_prompt(q12, k12_all, vd_all, bias_tiles, lam_params, subln_g, layer, batch, seq, lam_init):
    tile = min(DIFF_TILE, seq)
    assert tile >= MAX_DISTANCE
    nq = seq // tile
    nh = DIFF_HEADS_PER_STEP
    width = nh * LANES
    body = functools.partial(_diff_prompt_body, tile=tile, lam_init=lam_init, n_heads=nh)
    qmap = lambda b, h, i: (b * nq + i, h)
    kvmap = lambda b, h, i: (layer, b, 0)
    return pl.pallas_call(
        body, grid=(batch, DIFF_HEADS // nh, nq),
        in_specs=[pl.BlockSpec((None, 4, 64), lambda b, h, i: (layer, 0, 0)),
                  pl.BlockSpec((None, 1, DIFF_V_DIM), lambda b, h, i: (layer, 0, 0)),
                  pl.BlockSpec((tile, width), qmap),
                  pl.BlockSpec((None, DIFF_HEADS * seq, LANES), kvmap),
                  pl.BlockSpec((None, DIFF_HEADS * seq, LANES), kvmap),
                  pl.BlockSpec((nh, 2, tile, tile), lambda b, h, i: (h, 0, 0, 0))],
        out_specs=pl.BlockSpec((tile, width), qmap),
        out_shape=jax.ShapeDtypeStruct((batch * seq, DIFF_V_WIDTH), BF16),
        scratch_shapes=[pltpu.VMEM((seq, width), BF16), pltpu.VMEM((nh, seq, 2 * LANES), BF16),
                        pltpu.VMEM((2 * nh, tile, 1), F32), pltpu.VMEM((2 * nh, tile, 2 * LANES), F32)],
        compiler_params=_params(("arbitrary", "arbitrary", "arbitrary")),
        name=f"diff_prompt_{layer}")(lam_params, subln_g, q12, k12_all, vd_all, bias_tiles)


def _rows_per_head(x):
    n_rows = 2 * DIFF_HEADS
    head = lax.broadcasted_iota(jnp.int32, (n_rows, LANES), 0) & (DIFF_HEADS - 1)
    out = jnp.zeros((n_rows, LANES), F32)
    for hh in range(DIFF_HEADS):
        out = jnp.where(head == hh, jnp.broadcast_to(x[:, hh * LANES:(hh + 1) * LANES], (n_rows, LANES)), out)
    return out


def _diff_sample_body(pt_ref, lam_ref, g_ref, bias_ref, q_ref, kn_ref, vn_ref, *rest,
                      n_pages_step, n_steps, lam_init):
    del pt_ref
    k_refs = rest[:n_pages_step]
    v_refs = rest[n_pages_step:2 * n_pages_step]
    o_ref, k_s, v_s, m_s, l_s, acc_s = rest[2 * n_pages_step:]
    g = pl.program_id(1)
    n_rows = 2 * DIFF_HEADS
    page_rows = DIFF_HEADS * PAGE_SIZE

    @pl.when(g == 0)
    def _():
        m_s[...] = jnp.full_like(m_s, NEG_INF)
        l_s[...] = jnp.zeros_like(l_s)
        acc_s[...] = jnp.zeros_like(acc_s)

    for p in range(n_pages_step):
        k_s[p * page_rows:(p + 1) * page_rows, :] = k_refs[p][...].astype(BF16)
        v_s[p * page_rows:(p + 1) * page_rows, :] = v_refs[p][...].astype(BF16)

    q8 = _rows_per_head(q_ref[...].astype(F32))
    r8 = lax.broadcasted_iota(jnp.int32, (n_rows, LANES), 0)
    l8 = lax.broadcasted_iota(jnp.int32, (n_rows, LANES), 1)
    q8 = jnp.where((l8 >> 6) == (r8 >> 2), q8, 0.0).astype(BF16)
    shape = (n_rows, n_pages_step * page_rows)
    row = lax.broadcasted_iota(jnp.int32, shape, 0)
    col = lax.broadcasted_iota(jnp.int32, shape, 1)
    own = (col & (DIFF_HEADS - 1)) == (row & (DIFF_HEADS - 1))
    s = _dot_nt(q8, k_s[...])
    last = jnp.where(g == n_steps - 1, bias_ref[0], 0.0)
    split = (n_pages_step - 1) * page_rows
    s = jnp.concatenate([s[:, :split], s[:, split:] + last], axis=1) if split else s + last
    s = jnp.where(own, s, NEG_INF)
    m, l, acc = _softmax_step(s, v_s[...], m_s[...], l_s[...], acc_s[...])
    m_s[...] = m
    l_s[...] = l
    acc_s[...] = acc

    @pl.when(g == n_steps - 1)
    def _():
        k_new = _rows_per_head(kn_ref[...]).astype(BF16).astype(F32)
        v_new = _rows_per_head(vn_ref[...]).astype(BF16).astype(F32)
        s_new = jnp.sum(q8.astype(F32) * k_new, axis=1, keepdims=True) + bias_ref[1][:, 0:1]
        m_new = jnp.maximum(m, s_new)
        a = jnp.exp(m - m_new)
        p_new = jnp.exp(s_new - m_new)
        l_f = a * l + p_new
        acc_f = a * acc + p_new.astype(BF16).astype(F32) * v_new
        d = acc_f / l_f
        lam = _diff_lambda(lam_ref[...], lam_init)
        o = d[0:DIFF_HEADS] - lam * d[DIFF_HEADS:n_rows]
        o_ref[...] = _sub_layer_norm(o, g_ref[...], lam_init).astype(BF16)


def _diff_sample(q12, k_new, v_new, cache_k, cache_v, page_table, bias_s, lam_params, subln_g, layer, lam_init):
    db = q12.shape[0]
    n_pages = page_table.shape[1]
    pps = min(DEC_PAGES, n_pages)
    n_steps = n_pages // pps
    n_rows = 2 * DIFF_HEADS
    width = DIFF_HEADS * PAGE_SIZE

    def page_spec(p):
        return pl.BlockSpec((None, None, width, LANES),
                            lambda b, g, pt: (layer, pt[b, g * pps + p], 0, 0))

    vec = pl.BlockSpec((None, 1, width), lambda b, g, pt: (b, 0, 0))
    grid_spec = pltpu.PrefetchScalarGridSpec(
        num_scalar_prefetch=1, grid=(db, n_steps),
        in_specs=[pl.BlockSpec((None, 4, 64), lambda b, g, pt: (layer, 0, 0)),
                  pl.BlockSpec((None, 1, DIFF_V_DIM), lambda b, g, pt: (layer, 0, 0)),
                  pl.BlockSpec((2, n_rows, width), lambda b, g, pt: (0, 0, 0)),
                  vec, vec, vec] + [page_spec(p) for p in range(pps)] * 2,
        out_specs=pl.BlockSpec((None, DIFF_HEADS, DIFF_V_DIM), lambda b, g, pt: (b, 0, 0)),
        scratch_shapes=[pltpu.VMEM((pps * width, LANES), BF16), pltpu.VMEM((pps * width, LANES), BF16),
                        pltpu.VMEM((n_rows, 1), F32), pltpu.VMEM((n_rows, 1), F32),
                        pltpu.VMEM((n_rows, DIFF_V_DIM), F32)])
    body = functools.partial(_diff_sample_body, n_pages_step=pps, n_steps=n_steps, lam_init=lam_init)
    out = pl.pallas_call(
        body, grid_spec=grid_spec, out_shape=jax.ShapeDtypeStruct((db, DIFF_HEADS, DIFF_V_DIM), BF16),
        compiler_params=_params(("arbitrary", "arbitrary")), name=f"diff_sample_{layer}")(
            page_table, lam_params, subln_g, bias_s, q12.reshape(db, 1, width), k_new.reshape(db, 1, width),
            v_new.reshape(db, 1, width), *([cache_k] * pps), *([cache_v] * pps))
    return out.reshape(db, DIFF_V_WIDTH)


ROUTE_E1, ROUTE_E2, ROUTE_RANK1, ROUTE_RANK2, ROUTE_W1, ROUTE_W2 = range(6)


def _to_row_tiles(ref, x):
    for c in range(x.shape[1] // LANES):
        ref[:, c, :] = x[:, c * LANES:(c + 1) * LANES]


def _from_row_tiles(ref):
    return jnp.concatenate([ref[:, c, :] for c in range(ref.shape[1])], axis=1)


def _merge_body(*refs, alpha, mode):
    if mode == "routed":
        (osb_ref, od_ref, gate_ref, x_ref, wpa_ref, wpb_ref, wo_ref, g_ref, b_ref, wr_ref,
         o_ref, xt_ref, route_ref, counts_ref, cnt_s) = refs
    elif mode == "gated":
        osb_ref, od_ref, gate_ref, x_ref, wpa_ref, wpb_ref, wo_ref, g_ref, b_ref, wr_ref, o_ref, gates_ref = refs
    else:
        osb_ref, od_ref, gate_ref, x_ref, wpa_ref, wpb_ref, wo_ref, g_ref, b_ref, o_ref = refs
    d = x_ref.shape[1]
    pa = _dot(osb_ref[...], wpa_ref[...])
    pb = _dot(od_ref[...], wpb_ref[...])
    mixed = gate_ref[:, 0:d].astype(F32) * pa + gate_ref[:, d:2 * d].astype(F32) * pb
    mix = _dot(mixed.astype(BF16), wo_ref[...])
    x1 = _layer_norm(alpha * x_ref[...] + mix, g_ref[...], b_ref[...])
    o_ref[...] = x1
    if mode == "plain":
        return
    logits = _dot(x1.astype(BF16), wr_ref[...])
    lane_i = lax.broadcasted_iota(jnp.int32, logits.shape, 1)
    logits = jnp.where(lane_i < N_EXPERTS, logits, -jnp.inf)
    lane = lane_i.astype(F32)
    v1 = jnp.max(logits, axis=1, keepdims=True)
    i1 = jnp.min(jnp.where(logits == v1, lane, float(LANES)), axis=1, keepdims=True)
    rest = jnp.where(lane == i1, -jnp.inf, logits)
    v2 = jnp.max(rest, axis=1, keepdims=True)
    i2 = jnp.min(jnp.where(rest == v2, lane, float(LANES)), axis=1, keepdims=True)
    e = jnp.exp(v2 - v1)
    w1 = 1.0 / (1.0 + e)
    w2 = e / (1.0 + e)
    if mode == "gated":
        gates_ref[...] = jnp.where(lane == i1, w1, 0.0) + jnp.where(lane == i2, w2, 0.0)
        return
    _to_row_tiles(xt_ref, x1)

    @pl.when(pl.program_id(0) == 0)
    def _():
        cnt_s[...] = jnp.zeros_like(cnt_s)

    tm = logits.shape[0]
    sel = jnp.where(lane == i1, 1.0, 0.0) + jnp.where(lane == i2, 1.0, 0.0)
    earlier = (lax.broadcasted_iota(jnp.int32, (tm, tm), 1) < lax.broadcasted_iota(jnp.int32, (tm, tm), 0))
    before = _dot(jnp.where(earlier, 1.0, 0.0).astype(BF16), sel.astype(BF16)) + cnt_s[...]
    rank1 = jnp.sum(jnp.where(lane == i1, before, 0.0), axis=1, keepdims=True)
    rank2 = jnp.sum(jnp.where(lane == i2, before, 0.0), axis=1, keepdims=True)
    cnt_s[...] += jnp.sum(sel, axis=0, keepdims=True)
    route = jnp.zeros_like(logits)
    for idx, val in ((ROUTE_E1, i1), (ROUTE_E2, i2), (ROUTE_RANK1, rank1), (ROUTE_RANK2, rank2),
                     (ROUTE_W1, w1), (ROUTE_W2, w2)):
        route = jnp.where(lane_i == idx, val, route)
    route_ref[...] = route
    counts_ref[...] = jnp.broadcast_to(cnt_s[...], counts_ref.shape)


def _merge(o_sb, o_d, gate, x, w_pa, w_pb, w_o, ln_g, ln_b, layer, alpha, mode="plain", router_w=None,
           moe_index=None):
    n, d = x.shape
    tm = min(ROW_TILE, n)
    row = lambda i: (i, 0)
    wsel = lambda i: (layer, 0, 0)
    vsel = lambda i: (layer, 0, 0)
    in_specs = [pl.BlockSpec((tm, SB_WIDTH), row), pl.BlockSpec((tm, DIFF_V_WIDTH), row),
                pl.BlockSpec((tm, 2 * d), row), pl.BlockSpec((tm, d), row),
                pl.BlockSpec((None, SB_WIDTH, d), wsel), pl.BlockSpec((None, DIFF_V_WIDTH, d), wsel),
                pl.BlockSpec((None, d, d), wsel), pl.BlockSpec((None, 1, d), vsel), pl.BlockSpec((None, 1, d), vsel)]
    args = [o_sb, o_d, gate, x, w_pa, w_pb, w_o, ln_g, ln_b]
    out_shape = [jax.ShapeDtypeStruct((n, d), F32)]
    out_specs = [pl.BlockSpec((tm, d), row)]
    scratch = []
    if mode != "plain":
        in_specs.append(pl.BlockSpec((None, d, LANES), lambda i: (moe_index, 0, 0)))
        args.append(router_w)
    if mode == "gated":
        out_shape.append(jax.ShapeDtypeStruct((n, LANES), F32))
        out_specs.append(pl.BlockSpec((tm, LANES), row))
    if mode == "routed":
        out_shape += [jax.ShapeDtypeStruct((n, d // LANES, LANES), F32), jax.ShapeDtypeStruct((n, LANES), F32),
                      jax.ShapeDtypeStruct((SUBLANES, LANES), F32)]
        out_specs += [pl.BlockSpec((tm, d // LANES, LANES), lambda i: (i, 0, 0)), pl.BlockSpec((tm, LANES), row),
                      pl.BlockSpec((SUBLANES, LANES), lambda i: (0, 0))]
        scratch = [pltpu.VMEM((1, LANES), F32)]
    body = functools.partial(_merge_body, alpha=alpha, mode=mode)
    return pl.pallas_call(
        body, grid=(n // tm,), in_specs=in_specs, out_specs=out_specs, out_shape=out_shape,
        scratch_shapes=scratch, compiler_params=_params(("arbitrary",)), name=f"merge_{layer}_{n}")(*args)


def _swiglu_hidden(xb, wg, wu):
    hg = _dot(xb, wg)
    hu = _dot(xb, wu)
    return (hg * jax.nn.sigmoid(hg) * hu).astype(BF16)


def _ffn_body(x_ref, wg_ref, wu_ref, wd_ref, g_ref, b_ref, o_ref, acc_s, xb_s, *, alpha, n_ff):
    j = pl.program_id(1)

    @pl.when(j == 0)
    def _():
        acc_s[...] = jnp.zeros_like(acc_s)
        xb_s[...] = x_ref[...].astype(BF16)

    h = _swiglu_hidden(xb_s[...], wg_ref[...], wu_ref[...])
    acc_s[...] += _dot(h, wd_ref[...])

    @pl.when(j == n_ff - 1)
    def _():
        o_ref[...] = _layer_norm(alpha * x_ref[...] + acc_s[...], g_ref[...], b_ref[...])


def _ffn_dense(x, wg, wu, wd, ln_g, ln_b, layer, index, alpha):
    n, d = x.shape
    f = wg.shape[2]
    tm = min(FFN_ROW_TILE, n)
    tf = _largest_tile(f, 256)
    n_ff = f // tf
    body = functools.partial(_ffn_body, alpha=alpha, n_ff=n_ff)
    return pl.pallas_call(
        body, grid=(n // tm, n_ff),
        in_specs=[pl.BlockSpec((tm, d), lambda i, j: (i, 0)),
                  pl.BlockSpec((None, d, tf), lambda i, j: (index, 0, j)),
                  pl.BlockSpec((None, d, tf), lambda i, j: (index, 0, j)),
                  pl.BlockSpec((None, tf, d), lambda i, j: (index, j, 0)),
                  pl.BlockSpec((None, 1, d), lambda i, j: (layer, 0, 0)),
                  pl.BlockSpec((None, 1, d), lambda i, j: (layer, 0, 0))],
        out_specs=pl.BlockSpec((tm, d), lambda i, j: (i, 0)),
        out_shape=jax.ShapeDtypeStruct((n, d), F32),
        scratch_shapes=[pltpu.VMEM((tm, d), F32), pltpu.VMEM((tm, d), BF16)],
        compiler_params=_params(("arbitrary", "arbitrary")), name=f"ffn_{layer}_{n}")(x, wg, wu, wd, ln_g, ln_b)


def _moe_body(x_ref, gates_ref, wg_ref, wu_ref, wd_ref, g_ref, b_ref, o_ref, acc_s, xb_s, *, alpha, n_ff):
    e = pl.program_id(1)
    j = pl.program_id(2)

    @pl.when(jnp.logical_and(e == 0, j == 0))
    def _():
        acc_s[...] = jnp.zeros_like(acc_s)
        xb_s[...] = x_ref[...].astype(BF16)

    h = _swiglu_hidden(xb_s[...], wg_ref[...].astype(BF16), wu_ref[...].astype(BF16))
    y = _dot(h, wd_ref[...].astype(BF16))
    gates = gates_ref[...]
    lane = lax.broadcasted_iota(jnp.int32, gates.shape, 1)
    gate_e = jnp.sum(jnp.where(lane == e, gates, 0.0), axis=1, keepdims=True)
    acc_s[...] += gate_e * y

    @pl.when(jnp.logical_and(e == N_EXPERTS - 1, j == n_ff - 1))
    def _():
        o_ref[...] = _layer_norm(alpha * x_ref[...] + acc_s[...], g_ref[...], b_ref[...])


def _ffn_moe(x, gates, wg, wu, wd, ln_g, ln_b, layer, index, alpha):
    n, d = x.shape
    f = wg.shape[3]
    tm = min(FFN_ROW_TILE, n)
    tf = _largest_tile(f, 256)
    n_ff = f // tf
    body = functools.partial(_moe_body, alpha=alpha, n_ff=n_ff)
    return pl.pallas_call(
        body, grid=(n // tm, N_EXPERTS, n_ff),
        in_specs=[pl.BlockSpec((tm, d), lambda i, e, j: (i, 0)),
                  pl.BlockSpec((tm, LANES), lambda i, e, j: (i, 0)),
                  pl.BlockSpec((None, None, d, tf), lambda i, e, j: (index, e, 0, j)),
                  pl.BlockSpec((None, None, d, tf), lambda i, e, j: (index, e, 0, j)),
                  pl.BlockSpec((None, None, tf, d), lambda i, e, j: (index, e, j, 0)),
                  pl.BlockSpec((None, 1, d), lambda i, e, j: (layer, 0, 0)),
                  pl.BlockSpec((None, 1, d), lambda i, e, j: (layer, 0, 0))],
        out_specs=pl.BlockSpec((tm, d), lambda i, e, j: (i, 0)),
        out_shape=jax.ShapeDtypeStruct((n, d), F32),
        scratch_shapes=[pltpu.VMEM((tm, d), F32), pltpu.VMEM((tm, d), BF16)],
        compiler_params=_params(("arbitrary", "arbitrary", "arbitrary")),
        name=f"moe_{layer}_{n}")(x, gates, wg, wu, wd, ln_g, ln_b)


def _routing_tables(route, counts, row_tile, tok_tile):
    n = route.shape[0]
    as_int = lambda lane: route[:, lane].astype(jnp.int32)
    e1, e2, rank1, rank2 = (as_int(k) for k in (ROUTE_E1, ROUTE_E2, ROUTE_RANK1, ROUTE_RANK2))
    cnt = counts[0, :N_EXPERTS].astype(jnp.int32)
    padded = (cnt + row_tile - 1) // row_tile * row_tile
    ends = jnp.cumsum(padded)
    starts = ends - padded
    pos1 = (jnp.take(starts, e1) + rank1).reshape(n // tok_tile, 1, tok_tile)
    pos2 = (jnp.take(starts, e2) + rank2).reshape(n // tok_tile, 1, tok_tile)
    pos = jnp.concatenate([pos1, pos2], axis=2)
    n_row_tiles = TOP_K * n // row_tile + N_EXPERTS
    tile_start = jnp.arange(n_row_tiles, dtype=jnp.int32) * row_tile
    tile_expert = jnp.minimum(jnp.sum(tile_start[:, None] >= ends[None, :], axis=1), N_EXPERTS - 1)
    tile_used = (tile_start < ends[N_EXPERTS - 1]).astype(jnp.int32)
    return pos, tile_expert.astype(jnp.int32), tile_used, n_row_tiles


def _row_copy_wait(src_hbm, dst, sem):
    pltpu.make_async_copy(src_hbm.at[0], dst.at[0], sem).wait()


def _dispatch_body(pos_ref, xt_ref, rows_in, rows_hbm, sem, *, tok_tile):
    del rows_in

    def issue(t, carry):
        pltpu.make_async_copy(xt_ref.at[t], rows_hbm.at[pos_ref[0, t]], sem).start()
        pltpu.make_async_copy(xt_ref.at[t], rows_hbm.at[pos_ref[0, tok_tile + t]], sem).start()
        return carry

    lax.fori_loop(0, tok_tile, issue, 0)

    def drain(t, carry):
        _row_copy_wait(xt_ref, rows_hbm, sem)
        _row_copy_wait(xt_ref, rows_hbm, sem)
        return carry

    lax.fori_loop(0, tok_tile, drain, 0)


def _dispatch(xt, pos, n_rows, tok_tile):
    n = xt.shape[0]
    rows0 = jnp.zeros((n_rows,) + xt.shape[1:], F32)
    return pl.pallas_call(
        functools.partial(_dispatch_body, tok_tile=tok_tile), grid=(n // tok_tile,),
        in_specs=[pl.BlockSpec((None, 1, 2 * tok_tile), lambda i: (i, 0, 0), memory_space=pltpu.SMEM),
                  pl.BlockSpec((tok_tile,) + xt.shape[1:], lambda i: (i, 0, 0)),
                  pl.BlockSpec(memory_space=pl.ANY)],
        out_specs=pl.BlockSpec(memory_space=pl.ANY),
        out_shape=jax.ShapeDtypeStruct(rows0.shape, F32),
        scratch_shapes=[pltpu.SemaphoreType.DMA(())],
        input_output_aliases={2: 0},
        compiler_params=_params(("arbitrary",)), name="moe_dispatch")(pos, xt, rows0)


def _expert_rows_body(te_ref, tu_ref, x_ref, wg_ref, wu_ref, wd_ref, y_ref, acc_s, xb_s, *, n_ff):
    del te_ref
    t = pl.program_id(0)
    j = pl.program_id(1)
    used = tu_ref[t] == 1

    @pl.when(j == 0)
    def _():
        acc_s[...] = jnp.zeros_like(acc_s)
        xb_s[...] = _from_row_tiles(x_ref).astype(BF16)

    @pl.when(used)
    def _():
        h = _swiglu_hidden(xb_s[...], wg_ref[...].astype(BF16), wu_ref[...].astype(BF16))
        acc_s[...] += _dot(h, wd_ref[...].astype(BF16))

    @pl.when(j == n_ff - 1)
    def _():
        _to_row_tiles(y_ref, acc_s[...])


def _expert_rows(rows, tile_expert, tile_used, wg, wu, wd, index, row_tile, n_row_tiles):
    d = rows.shape[1] * rows.shape[2]
    f = wg.shape[3]
    tf = _largest_tile(f, 256)
    n_ff = f // tf
    wcol = lambda t, j, te, tu: (index, te[t], 0, j * tu[t])
    wrow = lambda t, j, te, tu: (index, te[t], j * tu[t], 0)
    rmap = lambda t, j, te, tu: (t, 0, 0)
    grid_spec = pltpu.PrefetchScalarGridSpec(
        num_scalar_prefetch=2, grid=(n_row_tiles, n_ff),
        in_specs=[pl.BlockSpec((row_tile,) + rows.shape[1:], rmap),
                  pl.BlockSpec((None, None, d, tf), wcol), pl.BlockSpec((None, None, d, tf), wcol),
                  pl.BlockSpec((None, None, tf, d), wrow)],
        out_specs=pl.BlockSpec((row_tile,) + rows.shape[1:], rmap),
        scratch_shapes=[pltpu.VMEM((row_tile, d), F32), pltpu.VMEM((row_tile, d), BF16)])
    return pl.pallas_call(
        functools.partial(_expert_rows_body, n_ff=n_ff), grid_spec=grid_spec,
        out_shape=jax.ShapeDtypeStruct(rows.shape, F32),
        compiler_params=_params(("arbitrary", "arbitrary")), name="moe_experts")(
            tile_expert, tile_used, rows, wg, wu, wd)


def _combine_body(pos_ref, y_hbm, x_ref, route_ref, g_ref, b_ref, o_ref, buf, sem, *, tok_tile, alpha):
    def issue(t, carry):
        pltpu.make_async_copy(y_hbm.at[pos_ref[0, t]], buf.at[0, t], sem).start()
        pltpu.make_async_copy(y_hbm.at[pos_ref[0, tok_tile + t]], buf.at[1, t], sem).start()
        return carry

    lax.fori_loop(0, tok_tile, issue, 0)

    def drain(t, carry):
        _row_copy_wait(y_hbm, buf.at[0], sem)
        _row_copy_wait(y_hbm, buf.at[1], sem)
        return carry

    lax.fori_loop(0, tok_tile, drain, 0)
    route = route_ref[...]
    lane = lax.broadcasted_iota(jnp.int32, route.shape, 1)
    w1 = jnp.sum(jnp.where(lane == ROUTE_W1, route, 0.0), axis=1, keepdims=True)
    w2 = jnp.sum(jnp.where(lane == ROUTE_W2, route, 0.0), axis=1, keepdims=True)
    y = w1 * _from_row_tiles(buf.at[0]) + w2 * _from_row_tiles(buf.at[1])
    o_ref[...] = _layer_norm(alpha * x_ref[...] + y, g_ref[...], b_ref[...])


def _combine(y_rows, pos, x, route, ln_g, ln_b, layer, alpha, tok_tile):
    n, d = x.shape
    row = lambda i: (i, 0)
    return pl.pallas_call(
        functools.partial(_combine_body, tok_tile=tok_tile, alpha=alpha), grid=(n // tok_tile,),
        in_specs=[pl.BlockSpec((None, 1, 2 * tok_tile), lambda i: (i, 0, 0), memory_space=pltpu.SMEM),
                  pl.BlockSpec(memory_space=pl.ANY),
                  pl.BlockSpec((tok_tile, d), row), pl.BlockSpec((tok_tile, LANES), row),
                  pl.BlockSpec((None, 1, d), lambda i: (layer, 0, 0)),
                  pl.BlockSpec((None, 1, d), lambda i: (layer, 0, 0))],
        out_specs=pl.BlockSpec((tok_tile, d), row),
        out_shape=jax.ShapeDtypeStruct((n, d), F32),
        scratch_shapes=[pltpu.VMEM((TOP_K, tok_tile) + y_rows.shape[1:], F32), pltpu.SemaphoreType.DMA(())],
        compiler_params=_params(("arbitrary",)), name=f"moe_combine_{layer}")(pos, y_rows, x, route, ln_g, ln_b)


def _ffn_moe_routed(x, xt, route, counts, wg, wu, wd, ln_g, ln_b, layer, index, alpha):
    n = x.shape[0]
    row_tile = min(MOE_ROW_TILE, n)
    tok_tile = min(MOE_TOKEN_TILE, n)
    pos, tile_expert, tile_used, n_row_tiles = _routing_tables(route, counts, row_tile, tok_tile)
    rows = _dispatch(xt, pos, n_row_tiles * row_tile, tok_tile)
    y_rows = _expert_rows(rows, tile_expert, tile_used, wg, wu, wd, index, row_tile, n_row_tiles)
    return _combine(y_rows, pos, x, route, ln_g, ln_b, layer, alpha, tok_tile)


def _prepare_in_proj(w_in):
    depth, d, _ = w_in.shape

    def pair(a, b):
        a = a.reshape(depth, d, DIFF_HEADS, 64)
        b = b.reshape(depth, d, DIFF_HEADS, 64)
        return jnp.concatenate([a, b], axis=-1).reshape(depth, d, 2 * DIFF_QK_WIDTH)

    c = [0, 512, 1024, 1536, 1792, 2048, 2304, 2560, 3072, 3072 + 2 * d]
    part = [w_in[:, :, c[k]:c[k + 1]] for k in range(9)]
    q_sb, k_sb, v_sb, q1, q2, k1, k2, v_d, gate = part
    w_main = jnp.concatenate([q_sb, pair(q1, q2), pair(k1, k2), v_d, gate], axis=-1).astype(BF16)
    w_kvt = jnp.swapaxes(jnp.concatenate([k_sb, v_sb], axis=-1), 1, 2).astype(BF16)
    return w_main, w_kvt


def kernel(x_prompt, x_sample, cache_sb_k, cache_sb_v, cache_diff_k, cache_diff_v, page_table, rel_bias, w_in,
           w_pa, w_pb, w_o, diff_lambda, diff_subln_g, ln1_g, ln1_b, ln2_g, ln2_b, ffn_w_gate, ffn_w_up,
           ffn_w_down, router_w, moe_w_gate, moe_w_up, moe_w_down):
    batch, seq, d = x_prompt.shape
    db = x_sample.shape[0]
    depth = w_in.shape[0]
    n_pool = cache_sb_k.shape[1]
    alpha = (2.0 * depth) ** 0.25
    n = batch * seq

    w_main, w_kvt = _prepare_in_proj(w_in)
    w_pa_b, w_pb_b, w_o_b = w_pa.astype(BF16), w_pb.astype(BF16), w_o.astype(BF16)
    ffn_g_b, ffn_u_b, ffn_d_b = ffn_w_gate.astype(BF16), ffn_w_up.astype(BF16), ffn_w_down.astype(BF16)
    router_b = jnp.pad(router_w, ((0, 0), (0, 0), (0, LANES - N_EXPERTS))).astype(BF16)

    sbk_t = jnp.transpose(cache_sb_k, (0, 1, 3, 4, 2)).reshape(depth, n_pool, SB_WIDTH, PAGE_SIZE)
    sbv_t = jnp.transpose(cache_sb_v, (0, 1, 3, 4, 2)).reshape(depth, n_pool, SB_WIDTH, PAGE_SIZE)
    dk_r = cache_diff_k.reshape(depth, n_pool, PAGE_SIZE * DIFF_HEADS, LANES)
    dv_r = cache_diff_v.reshape(depth, n_pool, PAGE_SIZE * DIFF_HEADS, LANES)

    ln1_g, ln1_b, ln2_g, ln2_b = (v.reshape(depth, 1, d) for v in (ln1_g, ln1_b, ln2_g, ln2_b))
    diff_subln_g = diff_subln_g.reshape(depth, 1, DIFF_V_DIM)

    bias_t = _bias_tiles(rel_bias, min(DIFF_TILE, seq))
    bias_s = _bias_sample(rel_bias)

    xp = x_prompt.reshape(n, d)
    xs = x_sample.reshape(db, d)
    stacked = None
    s_ksb, s_vsb, s_k12, s_vd = [], [], [], []
    for l in range(depth):
        lam_init = 0.8 - 0.6 * math.exp(-0.3 * l)
        is_moe = l % 2 == 1
        idx = l // 2

        q_sb, q12, gate, k12_all, vd_all, kt_all, vt_all = _proj_prompt(xp, w_main, w_kvt, l, stacked, batch, seq)
        stacked = (k12_all, vd_all, kt_all, vt_all)
        o_sb = _sb_prompt(q_sb, kt_all, vt_all, l, batch, seq)
        o_d = _diff_prompt(q12, k12_all, vd_all, bias_t, diff_lambda, diff_subln_g, l, batch, seq, lam_init)
        if is_moe:
            xp, xt, route, counts = _merge(o_sb, o_d, gate, xp, w_pa_b, w_pb_b, w_o_b, ln1_g, ln1_b, l, alpha,
                                           "routed", router_b, idx)
            xp = _ffn_moe_routed(xp, xt, route, counts, moe_w_gate, moe_w_up, moe_w_down, ln2_g, ln2_b, l, idx,
                                 alpha)
        else:
            (xp,) = _merge(o_sb, o_d, gate, xp, w_pa_b, w_pb_b, w_o_b, ln1_g, ln1_b, l, alpha)
            xp = _ffn_dense(xp, ffn_g_b, ffn_u_b, ffn_d_b, ln2_g, ln2_b, l, idx, alpha)

        sq_sb, sq12, sgate, sk12, svd, sksb, svsb = _proj_sample(xs, w_main, w_kvt, l)
        so_sb = _sb_sample(sq_sb, sbk_t, sbv_t, page_table, l)
        so_d = _diff_sample(sq12, sk12, svd, dk_r, dv_r, page_table, bias_s, diff_lambda, diff_subln_g, l, lam_init)
        if is_moe:
            xs, sgates = _merge(so_sb, so_d, sgate, xs, w_pa_b, w_pb_b, w_o_b, ln1_g, ln1_b, l, alpha,
                                "gated", router_b, idx)
            xs = _ffn_moe(xs, sgates, moe_w_gate, moe_w_up, moe_w_down, ln2_g, ln2_b, l, idx, alpha)
        else:
            (xs,) = _merge(so_sb, so_d, sgate, xs, w_pa_b, w_pb_b, w_o_b, ln1_g, ln1_b, l, alpha)
            xs = _ffn_dense(xs, ffn_g_b, ffn_u_b, ffn_d_b, ln2_g, ln2_b, l, idx, alpha)
        s_ksb.append(sksb)
        s_vsb.append(svsb)
        s_k12.append(sk12)
        s_vd.append(svd)

    k12_all, vd_all, kt_all, vt_all = stacked
    to_sb = lambda t: jnp.transpose(t.reshape(depth, batch, SB_HEADS, 64, seq), (0, 1, 4, 2, 3))
    to_diff = lambda t: t.reshape(depth, batch, seq, DIFF_HEADS, DIFF_V_DIM)
    return (xp.reshape(batch, seq, d), xs.reshape(db, 1, d),
            to_sb(kt_all), to_sb(vt_all), to_diff(k12_all), to_diff(vd_all),
            jnp.stack(s_ksb).reshape(depth, db, 1, SB_HEADS, 64),
            jnp.stack(s_vsb).reshape(depth, db, 1, SB_HEADS, 64),
            jnp.stack(s_k12).reshape(depth, db, 1, DIFF_HEADS, DIFF_V_DIM),
            jnp.stack(s_vd).reshape(depth, db, 1, DIFF_HEADS, DIFF_V_DIM))
```

```python
import functools
import math

import jax
import jax.numpy as jnp
from jax import lax
from jax.experimental import pallas as pl
from jax.experimental.pallas import tpu as pltpu

F32 = jnp.float32
BF16 = jnp.bfloat16

PAGE_SIZE = 128
SB_HEADS = 8
SB_WIDTH = 512
DIFF_HEADS = 4
DIFF_V_DIM = 128
DIFF_QK_WIDTH = 256
DIFF_V_WIDTH = 512
N_EXPERTS = 8
TOP_K = 2
N_BUCKETS = 32
MAX_DISTANCE = 128
LN_EPS = 1e-5
SUBLN_EPS = 1e-5
NEG_INF = -1e30
QK_SCALE = 0.125

LANES = 128
SUBLANES = 8
VMEM_LIMIT_BYTES = 56 * 1024 * 1024

ROW_TILE = 512
FFN_ROW_TILE = 1024
DIFF_TILE = 512
SB_TILE_Q = 512
SB_TILE_K = 512
SB_CUM_WIDTH = 256
DEC_PAGES = 8
SB_HEADS_PER_STEP = 4
DIFF_HEADS_PER_STEP = 2
MOE_ROW_TILE = 1024
MOE_TOKEN_TILE = 512

NT_DIMS = (((1,), (1,)), ((), ()))


def _params(semantics):
    return pltpu.CompilerParams(dimension_semantics=semantics, vmem_limit_bytes=VMEM_LIMIT_BYTES)


def _dot(a, b):
    return jnp.dot(a, b, preferred_element_type=F32)


def _dot_nt(a, b):
    return lax.dot_general(a, b, NT_DIMS, preferred_element_type=F32)


def _layer_norm(y, g, b):
    mu = jnp.mean(y, axis=-1, keepdims=True)
    d = y - mu
    var = jnp.mean(d * d, axis=-1, keepdims=True)
    return d * lax.rsqrt(var + LN_EPS) * g + b


def _strict_lower_ones(n):
    j = lax.broadcasted_iota(jnp.int32, (n, n), 0)
    s = lax.broadcasted_iota(jnp.int32, (n, n), 1)
    return jnp.where(j > s, 1.0, 0.0).astype(BF16)


def _largest_tile(dim, cap):
    if dim <= cap:
        return dim
    best = None
    for t in range(LANES, cap + 1, LANES):
        if dim % t == 0:
            best = t
    assert best is not None, (dim, cap)
    return best


def _store_head_rows(ref, x):
    rows = x.shape[0]
    for hd in range(DIFF_HEADS):
        ref[pl.ds(hd, rows, stride=DIFF_HEADS), :] = x[:, hd * LANES:(hd + 1) * LANES]


def _proj_prompt_body(x_ref, wm_ref, wkvt_ref, *rest):
    qsb_ref, q12_ref, gate_ref, k12_ref, vd_ref, ksbt_ref, vsbt_ref = rest[-7:]
    x = x_ref[...].astype(BF16)

    def cols(c0, n):
        return _dot(x, wm_ref[:, c0:c0 + n])

    qsb_ref[...] = (cols(0, 512) * QK_SCALE).astype(BF16)
    q12_ref[...] = (cols(512, 512) * QK_SCALE).astype(BF16)
    _store_head_rows(k12_ref, cols(1024, 512))
    _store_head_rows(vd_ref, cols(1536, 512))
    gate_ref[...] = jax.nn.sigmoid(cols(2048, 2048)).astype(BF16)
    ksbt_ref[...] = _dot_nt(wkvt_ref[0:512, :], x)
    vsbt_ref[...] = _dot_nt(wkvt_ref[512:1024, :], x)


def _proj_prompt(x, w_main, w_kvt, layer, stacked, batch, seq):
    n, d = x.shape
    depth = w_main.shape[0]
    tm = min(ROW_TILE, seq)
    nt = seq // tm
    grid = (n // tm,)
    row = lambda i: (i, 0)
    in_specs = [
        pl.BlockSpec((tm, d), row),
        pl.BlockSpec((None, d, w_main.shape[2]), lambda i: (layer, 0, 0)),
        pl.BlockSpec((None, w_kvt.shape[1], d), lambda i: (layer, 0, 0)),
    ]
    args = [x, w_main, w_kvt]
    aliases = {}
    if stacked is not None:
        in_specs += [pl.BlockSpec(memory_space=pl.ANY)] * 4
        args += list(stacked)
        aliases = {3: 3, 4: 4, 5: 5, 6: 6}
    out_shape = [
        jax.ShapeDtypeStruct((n, 512), BF16),
        jax.ShapeDtypeStruct((n, 512), BF16),
        jax.ShapeDtypeStruct((n, 2048), BF16),
        jax.ShapeDtypeStruct((depth, DIFF_HEADS * n, LANES), F32),
        jax.ShapeDtypeStruct((depth, DIFF_HEADS * n, LANES), F32),
        jax.ShapeDtypeStruct((depth, batch, 512, seq), F32),
        jax.ShapeDtypeStruct((depth, batch, 512, seq), F32),
    ]
    out_specs = [
        pl.BlockSpec((tm, 512), row),
        pl.BlockSpec((tm, 512), row),
        pl.BlockSpec((tm, 2048), row),
        pl.BlockSpec((None, DIFF_HEADS * tm, LANES), lambda i: (layer, i, 0)),
        pl.BlockSpec((None, DIFF_HEADS * tm, LANES), lambda i: (layer, i, 0)),
        pl.BlockSpec((None, None, 512, tm), lambda i: (layer, i // nt, 0, i % nt)),
        pl.BlockSpec((None, None, 512, tm), lambda i: (layer, i // nt, 0, i % nt)),
    ]
    return pl.pallas_call(
        _proj_prompt_body, grid=grid, in_specs=in_specs, out_specs=out_specs, out_shape=out_shape,
        input_output_aliases=aliases, compiler_params=_params(("arbitrary",)),
        name=f"proj_prompt_{layer}")(*args)


def _proj_sample_body(x_ref, wm_ref, wkvt_ref, qsb_ref, q12_ref, gate_ref, k12_ref, vd_ref, ksb_ref, vsb_ref):
    x = x_ref[...].astype(BF16)

    def cols(c0, n):
        return _dot(x, wm_ref[:, c0:c0 + n])

    qsb_ref[...] = (cols(0, 512) * QK_SCALE).astype(BF16)
    q12_ref[...] = (cols(512, 512) * QK_SCALE).astype(BF16)
    k12_ref[...] = cols(1024, 512)
    vd_ref[...] = cols(1536, 512)
    gate_ref[...] = jax.nn.sigmoid(cols(2048, 2048)).astype(BF16)
    ksb_ref[...] = _dot_nt(x, wkvt_ref[0:512, :])
    vsb_ref[...] = _dot_nt(x, wkvt_ref[512:1024, :])


def _proj_sample(x, w_main, w_kvt, layer):
    n, d = x.shape
    full = lambda shape: pl.BlockSpec(shape, lambda i: (0,) * len(shape))
    out_shape = [
        jax.ShapeDtypeStruct((n, 512), BF16), jax.ShapeDtypeStruct((n, 512), BF16),
        jax.ShapeDtypeStruct((n, 2048), BF16),
        jax.ShapeDtypeStruct((n, 512), F32), jax.ShapeDtypeStruct((n, 512), F32),
        jax.ShapeDtypeStruct((n, 512), F32), jax.ShapeDtypeStruct((n, 512), F32),
    ]
    return pl.pallas_call(
        _proj_sample_body, grid=(1,),
        in_specs=[full((n, d)),
                  pl.BlockSpec((None, d, w_main.shape[2]), lambda i: (layer, 0, 0)),
                  pl.BlockSpec((None, w_kvt.shape[1], d), lambda i: (layer, 0, 0))],
        out_specs=[full(s.shape) for s in out_shape], out_shape=out_shape,
        compiler_params=_params(("arbitrary",)), name=f"proj_sample_{layer}")(x, w_main, w_kvt)


def _softplus(z):
    return jnp.maximum(z, 0.0) + jnp.log(1.0 + jnp.exp(-jnp.abs(z)))


def _neg_suffix_sums(sp, neg_tri):
    w = neg_tri.shape[0]
    parts = []
    carry = None
    for c in reversed(range(sp.shape[1] // w)):
        blk = sp[:, c * w:(c + 1) * w]
        hi = blk.astype(BF16)
        lo = (blk - hi.astype(F32)).astype(BF16)
        cum = _dot(hi, neg_tri) + _dot(lo, neg_tri)
        if carry is not None:
            cum = cum + carry
        parts.insert(0, cum)
        carry = cum[:, 0:1] - blk[:, 0:1]
    return parts[0] if len(parts) == 1 else jnp.concatenate(parts, axis=1)


def _sb_block(q, k_t, v_t, neg_tri, r, acc, causal):
    z = _dot(q, k_t)
    sp = _softplus(z)
    log_beta = z - sp
    if causal is not None:
        sp = jnp.where(causal, sp, 0.0)
    cum = _neg_suffix_sums(sp, neg_tri)
    w = jnp.exp(log_beta + cum + r)
    if causal is not None:
        w = jnp.where(causal, w, 0.0)
    acc = acc + _dot_nt(w.astype(BF16), v_t)
    r = r + cum[:, 0:1] - sp[:, 0:1]
    return r, acc


def _sb_prompt_body(q_ref, kt_ref, vt_ref, o_ref, kt_s, vt_s, r_s, acc_s, *, tq, tk, n_blocks, n_pairs):
    i = pl.program_id(2)
    ratio = tq // tk

    @pl.when(i == 0)
    def _():
        for c in range(n_blocks):
            kt_s[c] = kt_ref[:, c * tk:(c + 1) * tk].astype(BF16)
            vt_s[c] = vt_ref[:, c * tk:(c + 1) * tk].astype(BF16)

    lane = lax.broadcasted_iota(jnp.int32, (tq, LANES), 1)
    q_heads = []
    for hp in range(n_pairs):
        q2 = q_ref[:, hp * LANES:(hp + 1) * LANES]
        zero = jnp.zeros_like(q2)
        q_heads += [jnp.where(lane < 64, q2, zero), jnp.where(lane >= 64, q2, zero)]
    row = lax.broadcasted_iota(jnp.int32, (tq, tk), 0)
    col = lax.broadcasted_iota(jnp.int32, (tq, tk), 1)
    neg_tri = -_strict_lower_ones(min(SB_CUM_WIDTH, tk))
    r_s[...] = jnp.zeros_like(r_s)
    acc_s[...] = jnp.zeros_like(acc_s)

    def step(kb, mask):
        for h in range(2 * n_pairs):
            rows = slice((h // 2) * LANES, (h // 2 + 1) * LANES)
            r, acc = _sb_block(q_heads[h], kt_s[kb, rows, :], vt_s[kb, rows, :], neg_tri, r_s[h], acc_s[h], mask)
            r_s[h] = r
            acc_s[h] = acc

    for d in reversed(range(ratio)):
        step(i * ratio + d, col + d * tk < row)

    def body(j, carry):
        step(i * ratio - 1 - j, None)
        return carry

    lax.fori_loop(0, i * ratio, body, 0)
    for hp in range(n_pairs):
        o_ref[:, hp * LANES:(hp + 1) * LANES] = jnp.where(lane < 64, acc_s[2 * hp], acc_s[2 * hp + 1]).astype(BF16)


def _sb_prompt(q, kt_all, vt_all, layer, batch, seq):
    tq = min(SB_TILE_Q, seq)
    tk = min(SB_TILE_K, tq)
    nq = seq // tq
    n_pairs = SB_HEADS_PER_STEP // 2
    width = n_pairs * LANES
    body = functools.partial(_sb_prompt_body, tq=tq, tk=tk, n_blocks=seq // tk, n_pairs=n_pairs)
    qmap = lambda b, h, i: (b * nq + i, h)
    kvmap = lambda b, h, i: (layer, b, h, 0)
    return pl.pallas_call(
        body, grid=(batch, SB_HEADS // SB_HEADS_PER_STEP, nq),
        in_specs=[pl.BlockSpec((tq, width), qmap),
                  pl.BlockSpec((None, None, width, seq), kvmap),
                  pl.BlockSpec((None, None, width, seq), kvmap)],
        out_specs=pl.BlockSpec((tq, width), qmap),
        out_shape=jax.ShapeDtypeStruct((batch * seq, SB_WIDTH), BF16),
        scratch_shapes=[pltpu.VMEM((seq // tk, width, tk), BF16), pltpu.VMEM((seq // tk, width, tk), BF16),
                        pltpu.VMEM((SB_HEADS_PER_STEP, tq, 1), F32),
                        pltpu.VMEM((SB_HEADS_PER_STEP, tq, LANES), F32)],
        compiler_params=_params(("arbitrary", "arbitrary", "arbitrary")),
        name=f"sb_prompt_{layer}")(q, kt_all, vt_all)


def _sb_sample_body(pt_ref, q_ref, *rest, n_pages_step, n_steps):
    del pt_ref
    k_refs = rest[:n_pages_step]
    v_refs = rest[n_pages_step:2 * n_pages_step]
    o_ref, kt_s, vt_s, r_s, acc_s = rest[2 * n_pages_step:]
    g = pl.program_id(1)
    page = lambda p: slice(p * PAGE_SIZE, (p + 1) * PAGE_SIZE)

    @pl.when(g == 0)
    def _():
        r_s[...] = jnp.zeros_like(r_s)
        acc_s[...] = jnp.zeros_like(acc_s)

    for p in range(n_pages_step):
        kt_s[:, page(p)] = k_refs[p][...].astype(BF16)
        vt_s[:, page(p)] = v_refs[p][...].astype(BF16)

    q = q_ref[...]
    row = lax.broadcasted_iota(jnp.int32, (SB_HEADS, SB_WIDTH), 0)
    col = lax.broadcasted_iota(jnp.int32, (SB_HEADS, SB_WIDTH), 1)
    own = (col >> 6) == row
    q_bd = jnp.where(own, jnp.broadcast_to(q.astype(F32), (SB_HEADS, SB_WIDTH)), 0.0).astype(BF16)
    z = _dot(q_bd, kt_s[...])
    sp = _softplus(z)
    log_beta = z - sp
    sp_rows = jnp.concatenate([sp[:, page(p)] for p in range(n_pages_step)], axis=0)
    cum_rows = _neg_suffix_sums(sp_rows, -_strict_lower_ones(PAGE_SIZE))
    r = r_s[...]
    w = [None] * n_pages_step
    for p in reversed(range(n_pages_step)):
        cum = cum_rows[p * SB_HEADS:(p + 1) * SB_HEADS]
        sp_p = sp[:, page(p)]
        w[p] = jnp.exp(log_beta[:, page(p)] + cum + r)
        r = r + jnp.broadcast_to(cum[:, 0:1] - sp_p[:, 0:1], r.shape)
    r_s[...] = r
    acc = acc_s[...] + _dot_nt(jnp.concatenate(w, axis=1).astype(BF16), vt_s[...])
    acc_s[...] = acc

    @pl.when(g == n_steps - 1)
    def _():
        o_ref[...] = jnp.sum(jnp.where(own, acc, 0.0), axis=0, keepdims=True).astype(BF16)


def _sb_sample(q, cache_kt, cache_vt, page_table, layer):
    db = q.shape[0]
    n_pages = page_table.shape[1]
    pps = min(DEC_PAGES, n_pages)
    n_steps = n_pages // pps
    q3 = q.reshape(db, 1, SB_WIDTH)

    def page_spec(p):
        return pl.BlockSpec((None, None, SB_WIDTH, PAGE_SIZE),
                            lambda b, g, pt: (layer, pt[b, (n_steps - 1 - g) * pps + p], 0, 0))

    qspec = pl.BlockSpec((None, 1, SB_WIDTH), lambda b, g, pt: (b, 0, 0))
    grid_spec = pltpu.PrefetchScalarGridSpec(
        num_scalar_prefetch=1, grid=(db, n_steps),
        in_specs=[qspec] + [page_spec(p) for p in range(pps)] * 2,
        out_specs=qspec,
        scratch_shapes=[pltpu.VMEM((SB_WIDTH, pps * PAGE_SIZE), BF16), pltpu.VMEM((SB_WIDTH, pps * PAGE_SIZE), BF16),
                        pltpu.VMEM((SB_HEADS, LANES), F32), pltpu.VMEM((SB_HEADS, SB_WIDTH), F32)])
    body = functools.partial(_sb_sample_body, n_pages_step=pps, n_steps=n_steps)
    out = pl.pallas_call(
        body, grid_spec=grid_spec, out_shape=jax.ShapeDtypeStruct((db, 1, SB_WIDTH), BF16),
        compiler_params=_params(("arbitrary", "arbitrary")), name=f"sb_sample_{layer}")(
            page_table, q3, *([cache_kt] * pps), *([cache_vt] * pps))
    return out.reshape(db, SB_WIDTH)


def _t5_bucket(n):
    max_exact = N_BUCKETS // 2
    nf = jnp.maximum(n, 1).astype(F32)
    large = max_exact + (jnp.log(nf / max_exact) / math.log(MAX_DISTANCE / max_exact)
                         * (N_BUCKETS - max_exact)).astype(jnp.int32)
    large = jnp.minimum(large, N_BUCKETS - 1)
    return jnp.where(n < max_exact, n, large)


def _bias_lookup(bucket, head, rb_ref, n_heads_static):
    val = jnp.zeros(bucket.shape, F32)
    for k in range(N_BUCKETS):
        if isinstance(head, int):
            val = jnp.where(bucket == k, rb_ref[k, head], val)
        else:
            for hh in range(n_heads_static):
                val = jnp.where((bucket == k) & (head == hh), rb_ref[k, hh], val)
    return val


def _bias_tiles_body(rb_ref, o_ref, *, tile):
    h = pl.program_id(0)
    which = pl.program_id(1)
    row = lax.broadcasted_iota(jnp.int32, (tile, tile), 0)
    col = lax.broadcasted_iota(jnp.int32, (tile, tile), 1)
    n = jnp.maximum(row - col + which * tile, 0)
    bucket = _t5_bucket(n)
    val = jnp.zeros((tile, tile), F32)
    for k in range(N_BUCKETS):
        val = jnp.where(bucket == k, rb_ref[k, h], val)
    o_ref[...] = val - rb_ref[N_BUCKETS - 1, h]


def _bias_tiles(rel_bias, tile):
    return pl.pallas_call(
        functools.partial(_bias_tiles_body, tile=tile), grid=(DIFF_HEADS, 2),
        in_specs=[pl.BlockSpec(memory_space=pltpu.SMEM)],
        out_specs=pl.BlockSpec((None, None, tile, tile), lambda h, w: (h, w, 0, 0)),
        out_shape=jax.ShapeDtypeStruct((DIFF_HEADS, 2, tile, tile), F32),
        compiler_params=_params(("arbitrary", "arbitrary")), name="bias_tiles")(rel_bias)


def _bias_sample_body(rb_ref, o_ref):
    shape = (2 * DIFF_HEADS, DIFF_HEADS * PAGE_SIZE)
    head = lax.broadcasted_iota(jnp.int32, shape, 0) & (DIFF_HEADS - 1)
    pos = lax.broadcasted_iota(jnp.int32, shape, 1) >> 2
    far = jnp.zeros(shape, jnp.int32) + MAX_DISTANCE
    last = PAGE_SIZE - pos
    new = jnp.zeros(shape, jnp.int32)
    far_bias = _bias_lookup(_t5_bucket(far), head, rb_ref, DIFF_HEADS)
    for idx, n in enumerate((last, new)):
        o_ref[idx] = _bias_lookup(_t5_bucket(n), head, rb_ref, DIFF_HEADS) - far_bias


def _bias_sample(rel_bias):
    shape = (2, 2 * DIFF_HEADS, DIFF_HEADS * PAGE_SIZE)
    return pl.pallas_call(
        _bias_sample_body, grid=(1,),
        in_specs=[pl.BlockSpec(memory_space=pltpu.SMEM)],
        out_specs=pl.BlockSpec(shape, lambda i: (0, 0, 0)),
        out_shape=jax.ShapeDtypeStruct(shape, F32),
        compiler_params=_params(("arbitrary",)), name="bias_sample")(rel_bias)


def _diff_lambda(lp, lam_init):
    a = jnp.sum(lp[0:1] * lp[1:2], axis=1, keepdims=True)
    b = jnp.sum(lp[2:3] * lp[3:4], axis=1, keepdims=True)
    return jnp.exp(a) - jnp.exp(b) + lam_init


def _softmax_step(s, v, m, l, acc):
    m_new = jnp.maximum(m, jnp.max(s, axis=1, keepdims=True))
    a = jnp.exp(m - m_new)
    p = jnp.exp(s - m_new)
    l = a * l + jnp.sum(p, axis=1, keepdims=True)
    acc = a * acc + _dot(p.astype(BF16), v)
    return m_new, l, acc


def _sub_layer_norm(o, g, lam_init):
    o = o * lax.rsqrt(jnp.mean(o * o, axis=-1, keepdims=True) + SUBLN_EPS)
    return o * g * (1.0 - lam_init)


def _diff_prompt_body(lam_ref, g_ref, q_ref, k_ref, v_ref, d_ref, o_ref,
                      k_s, v_s, m_s, acc_s, *, tile, lam_init, n_heads):
    i = pl.program_id(2)
    head = lambda hd: slice(hd * LANES, (hd + 1) * LANES)
    seq = k_s.shape[0]

    @pl.when(i == 0)
    def _():
        for hd in range(n_heads):
            rows = pl.ds(pl.program_id(1) * n_heads + hd, seq, stride=DIFF_HEADS)
            k_s[:, head(hd)] = k_ref[rows, :].astype(BF16)
            v_s[hd, :, 0:LANES] = v_ref[rows, :].astype(BF16)
            v_s[hd, :, LANES:2 * LANES] = jnp.ones((seq, LANES), BF16)

    lane = lax.broadcasted_iota(jnp.int32, (tile, LANES), 1)
    q_maps = []
    for hd in range(n_heads):
        q2 = q_ref[:, head(hd)]
        zero = jnp.zeros_like(q2)
        q_maps += [jnp.where(lane < 64, q2, zero), jnp.where(lane >= 64, q2, zero)]
    row = lax.broadcasted_iota(jnp.int32, (tile, tile), 0)
    col = lax.broadcasted_iota(jnp.int32, (tile, tile), 1)
    visible = col <= row
    m_s[...] = jnp.full_like(m_s, NEG_INF)
    acc_s[...] = jnp.zeros_like(acc_s)

    def step(kb, near, mask):
        start = pl.multiple_of(kb * tile, tile)
        for c in range(2 * n_heads):
            hd = c // 2
            s = _dot_nt(q_maps[c], k_s[pl.ds(start, tile), head(hd)])
            if near is not None:
                s = s + d_ref[hd, near]
            if mask is not None:
                s = jnp.where(mask, s, NEG_INF)
            m_old = m_s[c]
            m_new = jnp.maximum(m_old, jnp.max(s, axis=1, keepdims=True))
            p = jnp.exp(s - m_new)
            acc_s[c] = jnp.exp(m_old - m_new) * acc_s[c] + _dot(p.astype(BF16), v_s[hd, pl.ds(start, tile), :])
            m_s[c] = m_new

    step(i, 0, visible)

    @pl.when(i >= 1)
    def _():
        step(i - 1, 1, None)

    def body(j, carry):
        step(j, None, None)
        return carry

    lax.fori_loop(0, i - 1, body, 0)
    lam = _diff_lambda(lam_ref[...], lam_init)
    for hd in range(n_heads):
        a1, a2 = acc_s[2 * hd], acc_s[2 * hd + 1]
        o = a1[:, 0:LANES] / a1[:, LANES:2 * LANES] - lam * (a2[:, 0:LANES] / a2[:, LANES:2 * LANES])
        o_ref[:, head(hd)] = _sub_layer_norm(o, g_ref[...], lam_init).astype(BF16)


def _diff_prompt(q12, k12_all, vd_all, bias_tiles, lam_params, subln_g, layer, batch, seq, lam_init):
    tile = min(DIFF_TILE, seq)
    assert tile >= MAX_DISTANCE
    nq = seq // tile
    nh = DIFF_HEADS_PER_STEP
    width = nh * LANES
    body = functools.partial(_diff_prompt_body, tile=tile, lam_init=lam_init, n_heads=nh)
    qmap = lambda b, h, i: (b * nq + i, h)
    kvmap = lambda b, h, i: (layer, b, 0)
    return pl.pallas_call(
        body, grid=(batch, DIFF_HEADS // nh, nq),
        in_specs=[pl.BlockSpec((None, 4, 64), lambda b, h, i: (layer, 0, 0)),
                  pl.BlockSpec((None, 1, DIFF_V_DIM), lambda b, h, i: (layer, 0, 0)),
                  pl.BlockSpec((tile, width), qmap),
                  pl.BlockSpec((None, DIFF_HEADS * seq, LANES), kvmap),
                  pl.BlockSpec((None, DIFF_HEADS * seq, LANES), kvmap),
                  pl.BlockSpec((nh, 2, tile, tile), lambda b, h, i: (h, 0, 0, 0))],
        out_specs=pl.BlockSpec((tile, width), qmap),
        out_shape=jax.ShapeDtypeStruct((batch * seq, DIFF_V_WIDTH), BF16),
        scratch_shapes=[pltpu.VMEM((seq, width), BF16), pltpu.VMEM((nh, seq, 2 * LANES), BF16),
                        pltpu.VMEM((2 * nh, tile, 1), F32), pltpu.VMEM((2 * nh, tile, 2 * LANES), F32)],
        compiler_params=_params(("arbitrary", "arbitrary", "arbitrary")),
        name=f"diff_prompt_{layer}")(lam_params, subln_g, q12, k12_all, vd_all, bias_tiles)


def _rows_per_head(x):
    n_rows = 2 * DIFF_HEADS
    head = lax.broadcasted_iota(jnp.int32, (n_rows, LANES), 0) & (DIFF_HEADS - 1)
    out = jnp.zeros((n_rows, LANES), F32)
    for hh in range(DIFF_HEADS):
        out = jnp.where(head == hh, jnp.broadcast_to(x[:, hh * LANES:(hh + 1) * LANES], (n_rows, LANES)), out)
    return out


def _diff_sample_body(pt_ref, lam_ref, g_ref, bias_ref, q_ref, kn_ref, vn_ref, *rest,
                      n_pages_step, n_steps, lam_init):
    del pt_ref
    k_refs = rest[:n_pages_step]
    v_refs = rest[n_pages_step:2 * n_pages_step]
    o_ref, k_s, v_s, m_s, l_s, acc_s = rest[2 * n_pages_step:]
    g = pl.program_id(1)
    n_rows = 2 * DIFF_HEADS
    page_rows = DIFF_HEADS * PAGE_SIZE

    @pl.when(g == 0)
    def _():
        m_s[...] = jnp.full_like(m_s, NEG_INF)
        l_s[...] = jnp.zeros_like(l_s)
        acc_s[...] = jnp.zeros_like(acc_s)

    for p in range(n_pages_step):
        k_s[p * page_rows:(p + 1) * page_rows, :] = k_refs[p][...].astype(BF16)
        v_s[p * page_rows:(p + 1) * page_rows, :] = v_refs[p][...].astype(BF16)

    q8 = _rows_per_head(q_ref[...].astype(F32))
    r8 = lax.broadcasted_iota(jnp.int32, (n_rows, LANES), 0)
    l8 = lax.broadcasted_iota(jnp.int32, (n_rows, LANES), 1)
    q8 = jnp.where((l8 >> 6) == (r8 >> 2), q8, 0.0).astype(BF16)
    shape = (n_rows, n_pages_step * page_rows)
    row = lax.broadcasted_iota(jnp.int32, shape, 0)
    col = lax.broadcasted_iota(jnp.int32, shape, 1)
    own = (col & (DIFF_HEADS - 1)) == (row & (DIFF_HEADS - 1))
    s = _dot_nt(q8, k_s[...])
    last = jnp.where(g == n_steps - 1, bias_ref[0], 0.0)
    split = (n_pages_step - 1) * page_rows
    s = jnp.concatenate([s[:, :split], s[:, split:] + last], axis=1) if split else s + last
    s = jnp.where(own, s, NEG_INF)
    m, l, acc = _softmax_step(s, v_s[...], m_s[...], l_s[...], acc_s[...])
    m_s[...] = m
    l_s[...] = l
    acc_s[...] = acc

    @pl.when(g == n_steps - 1)
    def _():
        k_new = _rows_per_head(kn_ref[...]).astype(BF16).astype(F32)
        v_new = _rows_per_head(vn_ref[...]).astype(BF16).astype(F32)
        s_new = jnp.sum(q8.astype(F32) * k_new, axis=1, keepdims=True) + bias_ref[1][:, 0:1]
        m_new = jnp.maximum(m, s_new)
        a = jnp.exp(m - m_new)
        p_new = jnp.exp(s_new - m_new)
        l_f = a * l + p_new
        acc_f = a * acc + p_new.astype(BF16).astype(F32) * v_new
        d = acc_f / l_f
        lam = _diff_lambda(lam_ref[...], lam_init)
        o = d[0:DIFF_HEADS] - lam * d[DIFF_HEADS:n_rows]
        o_ref[...] = _sub_layer_norm(o, g_ref[...], lam_init).astype(BF16)


def _diff_sample(q12, k_new, v_new, cache_k, cache_v, page_table, bias_s, lam_params, subln_g, layer, lam_init):
    db = q12.shape[0]
    n_pages = page_table.shape[1]
    pps = min(DEC_PAGES, n_pages)
    n_steps = n_pages // pps
    n_rows = 2 * DIFF_HEADS
    width = DIFF_HEADS * PAGE_SIZE

    def page_spec(p):
        return pl.BlockSpec((None, None, width, LANES),
                            lambda b, g, pt: (layer, pt[b, g * pps + p], 0, 0))

    vec = pl.BlockSpec((None, 1, width), lambda b, g, pt: (b, 0, 0))
    grid_spec = pltpu.PrefetchScalarGridSpec(
        num_scalar_prefetch=1, grid=(db, n_steps),
        in_specs=[pl.BlockSpec((None, 4, 64), lambda b, g, pt: (layer, 0, 0)),
                  pl.BlockSpec((None, 1, DIFF_V_DIM), lambda b, g, pt: (layer, 0, 0)),
                  pl.BlockSpec((2, n_rows, width), lambda b, g, pt: (0, 0, 0)),
                  vec, vec, vec] + [page_spec(p) for p in range(pps)] * 2,
        out_specs=pl.BlockSpec((None, DIFF_HEADS, DIFF_V_DIM), lambda b, g, pt: (b, 0, 0)),
        scratch_shapes=[pltpu.VMEM((pps * width, LANES), BF16), pltpu.VMEM((pps * width, LANES), BF16),
                        pltpu.VMEM((n_rows, 1), F32), pltpu.VMEM((n_rows, 1), F32),
                        pltpu.VMEM((n_rows, DIFF_V_DIM), F32)])
    body = functools.partial(_diff_sample_body, n_pages_step=pps, n_steps=n_steps, lam_init=lam_init)
    out = pl.pallas_call(
        body, grid_spec=grid_spec, out_shape=jax.ShapeDtypeStruct((db, DIFF_HEADS, DIFF_V_DIM), BF16),
        compiler_params=_params(("arbitrary", "arbitrary")), name=f"diff_sample_{layer}")(
            page_table, lam_params, subln_g, bias_s, q12.reshape(db, 1, width), k_new.reshape(db, 1, width),
            v_new.reshape(db, 1, width), *([cache_k] * pps), *([cache_v] * pps))
    return out.reshape(db, DIFF_V_WIDTH)


ROUTE_E1, ROUTE_E2, ROUTE_RANK1, ROUTE_RANK2, ROUTE_W1, ROUTE_W2 = range(6)


def _to_row_tiles(ref, x):
    for c in range(x.shape[1] // LANES):
        ref[:, c, :] = x[:, c * LANES:(c + 1) * LANES]


def _from_row_tiles(ref):
    return jnp.concatenate([ref[:, c, :] for c in range(ref.shape[1])], axis=1)


def _merge_body(*refs, alpha, mode):
    if mode == "routed":
        (osb_ref, od_ref, gate_ref, x_ref, wpa_ref, wpb_ref, wo_ref, g_ref, b_ref, wr_ref,
         o_ref, xt_ref, route_ref, counts_ref, cnt_s) = refs
    elif mode == "gated":
        osb_ref, od_ref, gate_ref, x_ref, wpa_ref, wpb_ref, wo_ref, g_ref, b_ref, wr_ref, o_ref, gates_ref = refs
    else:
        osb_ref, od_ref, gate_ref, x_ref, wpa_ref, wpb_ref, wo_ref, g_ref, b_ref, o_ref = refs
    d = x_ref.shape[1]
    pa = _dot(osb_ref[...], wpa_ref[...])
    pb = _dot(od_ref[...], wpb_ref[...])
    mixed = gate_ref[:, 0:d].astype(F32) * pa + gate_ref[:, d:2 * d].astype(F32) * pb
    mix = _dot(mixed.astype(BF16), wo_ref[...])
    x1 = _layer_norm(alpha * x_ref[...] + mix, g_ref[...], b_ref[...])
    o_ref[...] = x1
    if mode == "plain":
        return
    logits = _dot(x1.astype(BF16), wr_ref[...])
    lane_i = lax.broadcasted_iota(jnp.int32, logits.shape, 1)
    logits = jnp.where(lane_i < N_EXPERTS, logits, -jnp.inf)
    lane = lane_i.astype(F32)
    v1 = jnp.max(logits, axis=1, keepdims=True)
    i1 = jnp.min(jnp.where(logits == v1, lane, float(LANES)), axis=1, keepdims=True)
    rest = jnp.where(lane == i1, -jnp.inf, logits)
    v2 = jnp.max(rest, axis=1, keepdims=True)
    i2 = jnp.min(jnp.where(rest == v2, lane, float(LANES)), axis=1, keepdims=True)
    e = jnp.exp(v2 - v1)
    w1 = 1.0 / (1.0 + e)
    w2 = e / (1.0 + e)
    if mode == "gated":
        gates_ref[...] = jnp.where(lane == i1, w1, 0.0) + jnp.where(lane == i2, w2, 0.0)
        return
    _to_row_tiles(xt_ref, x1)

    @pl.when(pl.program_id(0) == 0)
    def _():
        cnt_s[...] = jnp.zeros_like(cnt_s)

    tm = logits.shape[0]
    sel = jnp.where(lane == i1, 1.0, 0.0) + jnp.where(lane == i2, 1.0, 0.0)
    earlier = (lax.broadcasted_iota(jnp.int32, (tm, tm), 1) < lax.broadcasted_iota(jnp.int32, (tm, tm), 0))
    before = _dot(jnp.where(earlier, 1.0, 0.0).astype(BF16), sel.astype(BF16)) + cnt_s[...]
    rank1 = jnp.sum(jnp.where(lane == i1, before, 0.0), axis=1, keepdims=True)
    rank2 = jnp.sum(jnp.where(lane == i2, before, 0.0), axis=1, keepdims=True)
    cnt_s[...] += jnp.sum(sel, axis=0, keepdims=True)
    route = jnp.zeros_like(logits)
    for idx, val in ((ROUTE_E1, i1), (ROUTE_E2, i2), (ROUTE_RANK1, rank1), (ROUTE_RANK2, rank2),
                     (ROUTE_W1, w1), (ROUTE_W2, w2)):
        route = jnp.where(lane_i == idx, val, route)
    route_ref[...] = route
    counts_ref[...] = jnp.broadcast_to(cnt_s[...], counts_ref.shape)


def _merge(o_sb, o_d, gate, x, w_pa, w_pb, w_o, ln_g, ln_b, layer, alpha, mode="plain", router_w=None,
           moe_index=None):
    n, d = x.shape
    tm = min(ROW_TILE, n)
    row = lambda i: (i, 0)
    wsel = lambda i: (layer, 0, 0)
    vsel = lambda i: (layer, 0, 0)
    in_specs = [pl.BlockSpec((tm, SB_WIDTH), row), pl.BlockSpec((tm, DIFF_V_WIDTH), row),
                pl.BlockSpec((tm, 2 * d), row), pl.BlockSpec((tm, d), row),
                pl.BlockSpec((None, SB_WIDTH, d), wsel), pl.BlockSpec((None, DIFF_V_WIDTH, d), wsel),
                pl.BlockSpec((None, d, d), wsel), pl.BlockSpec((None, 1, d), vsel), pl.BlockSpec((None, 1, d), vsel)]
    args = [o_sb, o_d, gate, x, w_pa, w_pb, w_o, ln_g, ln_b]
    out_shape = [jax.ShapeDtypeStruct((n, d), F32)]
    out_specs = [pl.BlockSpec((tm, d), row)]
    scratch = []
    if mode != "plain":
        in_specs.append(pl.BlockSpec((None, d, LANES), lambda i: (moe_index, 0, 0)))
        args.append(router_w)
    if mode == "gated":
        out_shape.append(jax.ShapeDtypeStruct((n, LANES), F32))
        out_specs.append(pl.BlockSpec((tm, LANES), row))
    if mode == "routed":
        out_shape += [jax.ShapeDtypeStruct((n, d // LANES, LANES), F32), jax.ShapeDtypeStruct((n, LANES), F32),
                      jax.ShapeDtypeStruct((SUBLANES, LANES), F32)]
        out_specs += [pl.BlockSpec((tm, d // LANES, LANES), lambda i: (i, 0, 0)), pl.BlockSpec((tm, LANES), row),
                      pl.BlockSpec((SUBLANES, LANES), lambda i: (0, 0))]
        scratch = [pltpu.VMEM((1, LANES), F32)]
    body = functools.partial(_merge_body, alpha=alpha, mode=mode)
    return pl.pallas_call(
        body, grid=(n // tm,), in_specs=in_specs, out_specs=out_specs, out_shape=out_shape,
        scratch_shapes=scratch, compiler_params=_params(("arbitrary",)), name=f"merge_{layer}_{n}")(*args)


def _swiglu_hidden(xb, wg, wu):
    hg = _dot(xb, wg)
    hu = _dot(xb, wu)
    return (hg * jax.nn.sigmoid(hg) * hu).astype(BF16)


def _ffn_body(x_ref, wg_ref, wu_ref, wd_ref, g_ref, b_ref, o_ref, acc_s, xb_s, *, alpha, n_ff):
    j = pl.program_id(1)

    @pl.when(j == 0)
    def _():
        acc_s[...] = jnp.zeros_like(acc_s)
        xb_s[...] = x_ref[...].astype(BF16)

    h = _swiglu_hidden(xb_s[...], wg_ref[...], wu_ref[...])
    acc_s[...] += _dot(h, wd_ref[...])

    @pl.when(j == n_ff - 1)
    def _():
        o_ref[...] = _layer_norm(alpha * x_ref[...] + acc_s[...], g_ref[...], b_ref[...])


def _ffn_dense(x, wg, wu, wd, ln_g, ln_b, layer, index, alpha):
    n, d = x.shape
    f = wg.shape[2]
    tm = min(FFN_ROW_TILE, n)
    tf = _largest_tile(f, 256)
    n_ff = f // tf
    body = functools.partial(_ffn_body, alpha=alpha, n_ff=n_ff)
    return pl.pallas_call(
        body, grid=(n // tm, n_ff),
        in_specs=[pl.BlockSpec((tm, d), lambda i, j: (i, 0)),
                  pl.BlockSpec((None, d, tf), lambda i, j: (index, 0, j)),
                  pl.BlockSpec((None, d, tf), lambda i, j: (index, 0, j)),
                  pl.BlockSpec((None, tf, d), lambda i, j: (index, j, 0)),
                  pl.BlockSpec((None, 1, d), lambda i, j: (layer, 0, 0)),
                  pl.BlockSpec((None, 1, d), lambda i, j: (layer, 0, 0))],
        out_specs=pl.BlockSpec((tm, d), lambda i, j: (i, 0)),
        out_shape=jax.ShapeDtypeStruct((n, d), F32),
        scratch_shapes=[pltpu.VMEM((tm, d), F32), pltpu.VMEM((tm, d), BF16)],
        compiler_params=_params(("arbitrary", "arbitrary")), name=f"ffn_{layer}_{n}")(x, wg, wu, wd, ln_g, ln_b)


def _moe_body(x_ref, gates_ref, wg_ref, wu_ref, wd_ref, g_ref, b_ref, o_ref, acc_s, xb_s, *, alpha, n_ff):
    e = pl.program_id(1)
    j = pl.program_id(2)

    @pl.when(jnp.logical_and(e == 0, j == 0))
    def _():
        acc_s[...] = jnp.zeros_like(acc_s)
        xb_s[...] = x_ref[...].astype(BF16)

    h = _swiglu_hidden(xb_s[...], wg_ref[...].astype(BF16), wu_ref[...].astype(BF16))
    y = _dot(h, wd_ref[...].astype(BF16))
    gates = gates_ref[...]
    lane = lax.broadcasted_iota(jnp.int32, gates.shape, 1)
    gate_e = jnp.sum(jnp.where(lane == e, gates, 0.0), axis=1, keepdims=True)
    acc_s[...] += gate_e * y

    @pl.when(jnp.logical_and(e == N_EXPERTS - 1, j == n_ff - 1))
    def _():
        o_ref[...] = _layer_norm(alpha * x_ref[...] + acc_s[...], g_ref[...], b_ref[...])


def _ffn_moe(x, gates, wg, wu, wd, ln_g, ln_b, layer, index, alpha):
    n, d = x.shape
    f = wg.shape[3]
    tm = min(FFN_ROW_TILE, n)
    tf = _largest_tile(f, 256)
    n_ff = f // tf
    body = functools.partial(_moe_body, alpha=alpha, n_ff=n_ff)
    return pl.pallas_call(
        body, grid=(n // tm, N_EXPERTS, n_ff),
        in_specs=[pl.BlockSpec((tm, d), lambda i, e, j: (i, 0)),
                  pl.BlockSpec((tm, LANES), lambda i, e, j: (i, 0)),
                  pl.BlockSpec((None, None, d, tf), lambda i, e, j: (index, e, 0, j)),
                  pl.BlockSpec((None, None, d, tf), lambda i, e, j: (index, e, 0, j)),
                  pl.BlockSpec((None, None, tf, d), lambda i, e, j: (index, e, j, 0)),
                  pl.BlockSpec((None, 1, d), lambda i, e, j: (layer, 0, 0)),
                  pl.BlockSpec((None, 1, d), lambda i, e, j: (layer, 0, 0))],
        out_specs=pl.BlockSpec((tm, d), lambda i, e, j: (i, 0)),
        out_shape=jax.ShapeDtypeStruct((n, d), F32),
        scratch_shapes=[pltpu.VMEM((tm, d), F32), pltpu.VMEM((tm, d), BF16)],
        compiler_params=_params(("arbitrary", "arbitrary", "arbitrary")),
        name=f"moe_{layer}_{n}")(x, gates, wg, wu, wd, ln_g, ln_b)


def _routing_tables(route, counts, row_tile, tok_tile):
    n = route.shape[0]
    as_int = lambda lane: route[:, lane].astype(jnp.int32)
    e1, e2, rank1, rank2 = (as_int(k) for k in (ROUTE_E1, ROUTE_E2, ROUTE_RANK1, ROUTE_RANK2))
    cnt = counts[0, :N_EXPERTS].astype(jnp.int32)
    padded = (cnt + row_tile - 1) // row_tile * row_tile
    ends = jnp.cumsum(padded)
    starts = ends - padded
    pos1 = (jnp.take(starts, e1) + rank1).reshape(n // tok_tile, 1, tok_tile)
    pos2 = (jnp.take(starts, e2) + rank2).reshape(n // tok_tile, 1, tok_tile)
    pos = jnp.concatenate([pos1, pos2], axis=2)
    n_row_tiles = TOP_K * n // row_tile + N_EXPERTS
    tile_start = jnp.arange(n_row_tiles, dtype=jnp.int32) * row_tile
    tile_expert = jnp.minimum(jnp.sum(tile_start[:, None] >= ends[None, :], axis=1), N_EXPERTS - 1)
    tile_used = (tile_start < ends[N_EXPERTS - 1]).astype(jnp.int32)
    return pos, tile_expert.astype(jnp.int32), tile_used, n_row_tiles


def _row_copy_wait(src_hbm, dst, sem):
    pltpu.make_async_copy(src_hbm.at[0], dst.at[0], sem).wait()


def _dispatch_body(pos_ref, xt_ref, rows_in, rows_hbm, sem, *, tok_tile):
    del rows_in

    def issue(t, carry):
        pltpu.make_async_copy(xt_ref.at[t], rows_hbm.at[pos_ref[0, t]], sem).start(priority=0)
        pltpu.make_async_copy(xt_ref.at[t], rows_hbm.at[pos_ref[0, tok_tile + t]], sem).start(priority=1)
        return carry

    lax.fori_loop(0, tok_tile, issue, 0)

    def drain(t, carry):
        _row_copy_wait(xt_ref, rows_hbm, sem)
        _row_copy_wait(xt_ref, rows_hbm, sem)
        return carry

    lax.fori_loop(0, tok_tile, drain, 0)


def _dispatch(xt, pos, n_rows, tok_tile):
    n = xt.shape[0]
    rows0 = jnp.zeros((n_rows,) + xt.shape[1:], F32)
    return pl.pallas_call(
        functools.partial(_dispatch_body, tok_tile=tok_tile), grid=(n // tok_tile,),
        in_specs=[pl.BlockSpec((None, 1, 2 * tok_tile), lambda i: (i, 0, 0), memory_space=pltpu.SMEM),
                  pl.BlockSpec((tok_tile,) + xt.shape[1:], lambda i: (i, 0, 0)),
                  pl.BlockSpec(memory_space=pl.ANY)],
        out_specs=pl.BlockSpec(memory_space=pl.ANY),
        out_shape=jax.ShapeDtypeStruct(rows0.shape, F32),
        scratch_shapes=[pltpu.SemaphoreType.DMA(())],
        input_output_aliases={2: 0},
        compiler_params=_params(("arbitrary",)), name="moe_dispatch")(pos, xt, rows0)


def _expert_rows_body(te_ref, tu_ref, x_ref, wg_ref, wu_ref, wd_ref, y_ref, acc_s, xb_s, *, n_ff):
    del te_ref
    t = pl.program_id(0)
    j = pl.program_id(1)
    used = tu_ref[t] == 1

    @pl.when(j == 0)
    def _():
        acc_s[...] = jnp.zeros_like(acc_s)
        xb_s[...] = _from_row_tiles(x_ref).astype(BF16)

    @pl.when(used)
    def _():
        h = _swiglu_hidden(xb_s[...], wg_ref[...].astype(BF16), wu_ref[...].astype(BF16))
        acc_s[...] += _dot(h, wd_ref[...].astype(BF16))

    @pl.when(j == n_ff - 1)
    def _():
        _to_row_tiles(y_ref, acc_s[...])


def _expert_rows(rows, tile_expert, tile_used, wg, wu, wd, index, row_tile, n_row_tiles):
    d = rows.shape[1] * rows.shape[2]
    f = wg.shape[3]
    tf = _largest_tile(f, 256)
    n_ff = f // tf
    wcol = lambda t, j, te, tu: (index, te[t], 0, j * tu[t])
    wrow = lambda t, j, te, tu: (index, te[t], j * tu[t], 0)
    rmap = lambda t, j, te, tu: (t, 0, 0)
    grid_spec = pltpu.PrefetchScalarGridSpec(
        num_scalar_prefetch=2, grid=(n_row_tiles, n_ff),
        in_specs=[pl.BlockSpec((row_tile,) + rows.shape[1:], rmap),
                  pl.BlockSpec((None, None, d, tf), wcol), pl.BlockSpec((None, None, d, tf), wcol),
                  pl.BlockSpec((None, None, tf, d), wrow)],
        out_specs=pl.BlockSpec((row_tile,) + rows.shape[1:], rmap),
        scratch_shapes=[pltpu.VMEM((row_tile, d), F32), pltpu.VMEM((row_tile, d), BF16)])
    return pl.pallas_call(
        functools.partial(_expert_rows_body, n_ff=n_ff), grid_spec=grid_spec,
        out_shape=jax.ShapeDtypeStruct(rows.shape, F32),
        compiler_params=_params(("arbitrary", "arbitrary")), name="moe_experts")(
            tile_expert, tile_used, rows, wg, wu, wd)


def _combine_body(pos_ref, y_hbm, x_ref, route_ref, g_ref, b_ref, o_ref, buf, sem, *, tok_tile, alpha):
    def issue(t, carry):
        pltpu.make_async_copy(y_hbm.at[pos_ref[0, t]], buf.at[0, t], sem).start(priority=0)
        pltpu.make_async_copy(y_hbm.at[pos_ref[0, tok_tile + t]], buf.at[1, t], sem).start(priority=1)
        return carry

    lax.fori_loop(0, tok_tile, issue, 0)

    def drain(t, carry):
        _row_copy_wait(y_hbm, buf.at[0], sem)
        _row_copy_wait(y_hbm, buf.at[1], sem)
        return carry

    lax.fori_loop(0, tok_tile, drain, 0)
    route = route_ref[...]
    lane = lax.broadcasted_iota(jnp.int32, route.shape, 1)
    w1 = jnp.sum(jnp.where(lane == ROUTE_W1, route, 0.0), axis=1, keepdims=True)
    w2 = jnp.sum(jnp.where(lane == ROUTE_W2, route, 0.0), axis=1, keepdims=True)
    y = w1 * _from_row_tiles(buf.at[0]) + w2 * _from_row_tiles(buf.at[1])
    o_ref[...] = _layer_norm(alpha * x_ref[...] + y, g_ref[...], b_ref[...])


def _combine(y_rows, pos, x, route, ln_g, ln_b, layer, alpha, tok_tile):
    n, d = x.shape
    row = lambda i: (i, 0)
    return pl.pallas_call(
        functools.partial(_combine_body, tok_tile=tok_tile, alpha=alpha), grid=(n // tok_tile,),
        in_specs=[pl.BlockSpec((None, 1, 2 * tok_tile), lambda i: (i, 0, 0), memory_space=pltpu.SMEM),
                  pl.BlockSpec(memory_space=pl.ANY),
                  pl.BlockSpec((tok_tile, d), row), pl.BlockSpec((tok_tile, LANES), row),
                  pl.BlockSpec((None, 1, d), lambda i: (layer, 0, 0)),
                  pl.BlockSpec((None, 1, d), lambda i: (layer, 0, 0))],
        out_specs=pl.BlockSpec((tok_tile, d), row),
        out_shape=jax.ShapeDtypeStruct((n, d), F32),
        scratch_shapes=[pltpu.VMEM((TOP_K, tok_tile) + y_rows.shape[1:], F32), pltpu.SemaphoreType.DMA(())],
        compiler_params=_params(("arbitrary",)), name=f"moe_combine_{layer}")(pos, y_rows, x, route, ln_g, ln_b)


def _ffn_moe_routed(x, xt, route, counts, wg, wu, wd, ln_g, ln_b, layer, index, alpha):
    n = x.shape[0]
    row_tile = min(MOE_ROW_TILE, n)
    tok_tile = min(MOE_TOKEN_TILE, n)
    pos, tile_expert, tile_used, n_row_tiles = _routing_tables(route, counts, row_tile, tok_tile)
    rows = _dispatch(xt, pos, n_row_tiles * row_tile, tok_tile)
    y_rows = _expert_rows(rows, tile_expert, tile_used, wg, wu, wd, index, row_tile, n_row_tiles)
    return _combine(y_rows, pos, x, route, ln_g, ln_b, layer, alpha, tok_tile)


def _prepare_in_proj(w_in):
    depth, d, _ = w_in.shape

    def pair(a, b):
        a = a.reshape(depth, d, DIFF_HEADS, 64)
        b = b.reshape(depth, d, DIFF_HEADS, 64)
        return jnp.concatenate([a, b], axis=-1).reshape(depth, d, 2 * DIFF_QK_WIDTH)

    c = [0, 512, 1024, 1536, 1792, 2048, 2304, 2560, 3072, 3072 + 2 * d]
    part = [w_in[:, :, c[k]:c[k + 1]] for k in range(9)]
    q_sb, k_sb, v_sb, q1, q2, k1, k2, v_d, gate = part
    w_main = jnp.concatenate([q_sb, pair(q1, q2), pair(k1, k2), v_d, gate], axis=-1).astype(BF16)
    w_kvt = jnp.swapaxes(jnp.concatenate([k_sb, v_sb], axis=-1), 1, 2).astype(BF16)
    return w_main, w_kvt


def kernel(x_prompt, x_sample, cache_sb_k, cache_sb_v, cache_diff_k, cache_diff_v, page_table, rel_bias, w_in,
           w_pa, w_pb, w_o, diff_lambda, diff_subln_g, ln1_g, ln1_b, ln2_g, ln2_b, ffn_w_gate, ffn_w_up,
           ffn_w_down, router_w, moe_w_gate, moe_w_up, moe_w_down):
    batch, seq, d = x_prompt.shape
    db = x_sample.shape[0]
    depth = w_in.shape[0]
    n_pool = cache_sb_k.shape[1]
    alpha = (2.0 * depth) ** 0.25
    n = batch * seq

    w_main, w_kvt = _prepare_in_proj(w_in)
    w_pa_b, w_pb_b, w_o_b = w_pa.astype(BF16), w_pb.astype(BF16), w_o.astype(BF16)
    ffn_g_b, ffn_u_b, ffn_d_b = ffn_w_gate.astype(BF16), ffn_w_up.astype(BF16), ffn_w_down.astype(BF16)
    router_b = jnp.pad(router_w, ((0, 0), (0, 0), (0, LANES - N_EXPERTS))).astype(BF16)

    sbk_t = jnp.transpose(cache_sb_k, (0, 1, 3, 4, 2)).reshape(depth, n_pool, SB_WIDTH, PAGE_SIZE)
    sbv_t = jnp.transpose(cache_sb_v, (0, 1, 3, 4, 2)).reshape(depth, n_pool, SB_WIDTH, PAGE_SIZE)
    dk_r = cache_diff_k.reshape(depth, n_pool, PAGE_SIZE * DIFF_HEADS, LANES)
    dv_r = cache_diff_v.reshape(depth, n_pool, PAGE_SIZE * DIFF_HEADS, LANES)

    ln1_g, ln1_b, ln2_g, ln2_b = (v.reshape(depth, 1, d) for v in (ln1_g, ln1_b, ln2_g, ln2_b))
    diff_subln_g = diff_subln_g.reshape(depth, 1, DIFF_V_DIM)

    bias_t = _bias_tiles(rel_bias, min(DIFF_TILE, seq))
    bias_s = _bias_sample(rel_bias)

    xp = x_prompt.reshape(n, d)
    xs = x_sample.reshape(db, d)
    stacked = None
    s_ksb, s_vsb, s_k12, s_vd = [], [], [], []
    for l in range(depth):
        lam_init = 0.8 - 0.6 * math.exp(-0.3 * l)
        is_moe = l % 2 == 1
        idx = l // 2

        q_sb, q12, gate, k12_all, vd_all, kt_all, vt_all = _proj_prompt(xp, w_main, w_kvt, l, stacked, batch, seq)
        stacked = (k12_all, vd_all, kt_all, vt_all)
        o_sb = _sb_prompt(q_sb, kt_all, vt_all, l, batch, seq)
        o_d = _diff_prompt(q12, k12_all, vd_all, bias_t, diff_lambda, diff_subln_g, l, batch, seq, lam_init)
        if is_moe:
            xp, xt, route, counts = _merge(o_sb, o_d, gate, xp, w_pa_b, w_pb_b, w_o_b, ln1_g, ln1_b, l, alpha,
                                           "routed", router_b, idx)
            xp = _ffn_moe_routed(xp, xt, route, counts, moe_w_gate, moe_w_up, moe_w_down, ln2_g, ln2_b, l, idx,
                                 alpha)
        else:
            (xp,) = _merge(o_sb, o_d, gate, xp, w_pa_b, w_pb_b, w_o_b, ln1_g, ln1_b, l, alpha)
            xp = _ffn_dense(xp, ffn_g_b, ffn_u_b, ffn_d_b, ln2_g, ln2_b, l, idx, alpha)

        sq_sb, sq12, sgate, sk12, svd, sksb, svsb = _proj_sample(xs, w_main, w_kvt, l)
        so_sb = _sb_sample(sq_sb, sbk_t, sbv_t, page_table, l)
        so_d = _diff_sample(sq12, sk12, svd, dk_r, dv_r, page_table, bias_s, diff_lambda, diff_subln_g, l, lam_init)
        if is_moe:
            xs, sgates = _merge(so_sb, so_d, sgate, xs, w_pa_b, w_pb_b, w_o_b, ln1_g, ln1_b, l, alpha,
                                "gated", router_b, idx)
            xs = _ffn_moe(xs, sgates, moe_w_gate, moe_w_up, moe_w_down, ln2_g, ln2_b, l, idx, alpha)
        else:
            (xs,) = _merge(so_sb, so_d, sgate, xs, w_pa_b, w_pb_b, w_o_b, ln1_g, ln1_b, l, alpha)
            xs = _ffn_dense(xs, ffn_g_b, ffn_u_b, ffn_d_b, ln2_g, ln2_b, l, idx, alpha)
        s_ksb.append(sksb)
        s_vsb.append(svsb)
        s_k12.append(sk12)
        s_vd.append(svd)

    k12_all, vd_all, kt_all, vt_all = stacked
    to_sb = lambda t: jnp.transpose(t.reshape(depth, batch, SB_HEADS, 64, seq), (0, 1, 4, 2, 3))
    to_diff = lambda t: t.reshape(depth, batch, seq, DIFF_HEADS, DIFF_V_DIM)
    return (xp.reshape(batch, seq, d), xs.reshape(db, 1, d),
            to_sb(kt_all), to_sb(vt_all), to_diff(k12_all), to_diff(vd_all),
            jnp.stack(s_ksb).reshape(depth, db, 1, SB_HEADS, 64),
            jnp.stack(s_vsb).reshape(depth, db, 1, SB_HEADS, 64),
            jnp.stack(s_k12).reshape(depth, db, 1, DIFF_HEADS, DIFF_V_DIM),
            jnp.stack(s_vd).reshape(depth, db, 1, DIFF_HEADS, DIFF_V_DIM))
```
